```python
import math
import jax, jax.numpy as jnp
from jax import lax
import numpy as np

D_MODEL = 1024
BATCH = 16
SEQ = 2048
DEPTH = 4

MEM_LEN = 256
ROPE_THETA = 500000.0
Q_BLOCK = 128
MLA_HEADS = 8
MLA_NOPE = 64
MLA_ROPE = 32
MLA_V = 64
MLA_Q_LORA = 384
MLA_KV_LORA = 256
DIFF_HEADS = 8
DIFF_D = 64
DIFF_ROT = DIFF_D // 4
MEM_HEADS = 4
MEM_HD = 128
N_BRANCH = 3
D_FF = 4 * D_MODEL
DEEPNORM_ALPHA = (2 * DEPTH) ** 0.25
DEEPNORM_BETA = (8 * DEPTH) ** -0.25
LN_EPS = 1e-5

MLA_OUT = MLA_HEADS * MLA_V
DIFF_QK = DIFF_HEADS * 2 * DIFF_D
DIFF_OUT = DIFF_HEADS * 2 * DIFF_D
MEM_OUT = MEM_HEADS * MEM_HD
BRANCH_W = MLA_OUT + DIFF_OUT + MEM_OUT
IN_SIZES = (MLA_Q_LORA, MLA_KV_LORA, MLA_ROPE, DIFF_QK, DIFF_QK, DIFF_OUT, MEM_OUT, N_BRANCH * D_MODEL)
N_IN = sum(IN_SIZES)

kernel_name = 'hybrid_mla_diffattn_memory_encoder'


def _split_points():
    pts, acc = [], 0
    for s in IN_SIZES[:-1]:
        acc += s
        pts.append(acc)
    return pts


def _rms(x, g, eps=1e-6):
    xf = x.astype(jnp.float32)
    y = xf * lax.rsqrt(jnp.mean(xf * xf, axis=-1, keepdims=True) + eps)
    return (y * g.astype(jnp.float32)).astype(x.dtype)


def _layernorm(x, g, b):
    xf = x.astype(jnp.float32)
    mu = jnp.mean(xf, axis=-1, keepdims=True)
    var = jnp.mean(jnp.square(xf - mu), axis=-1, keepdims=True)
    y = (xf - mu) * lax.rsqrt(var + LN_EPS) * g.astype(jnp.float32) + b.astype(jnp.float32)
    return y.astype(x.dtype)


def _rope_tables(positions, rot_dim, dtype):
    inv = ROPE_THETA ** (-jnp.arange(0, rot_dim, 2, dtype=jnp.float32) / rot_dim)
    ang = positions.astype(jnp.float32)[..., None] * inv
    return jnp.cos(ang).astype(dtype), jnp.sin(ang).astype(dtype)


def _apply_rope(x, cos, sin):
    half = cos.shape[-1]
    shape = cos.shape[:2] + (1,) * (x.ndim - 3) + (half,)
    c, s = cos.reshape(shape), sin.reshape(shape)
    x1, x2, rest = x[..., :half], x[..., half:2 * half], x[..., 2 * half:]
    return jnp.concatenate([x1 * c - x2 * s, x2 * c + x1 * s, rest], axis=-1)


def _sweep_query_blocks(fn, q):
    b, s = q.shape[0], q.shape[1]
    nb = s // Q_BLOCK
    qb = jnp.moveaxis(q.reshape((b, nb, Q_BLOCK) + q.shape[2:]), 1, 0)
    out = jnp.moveaxis(lax.map(fn, qb), 0, 1)
    return out.reshape((b, s) + out.shape[3:])


def _mla_branch(c_q, c_kv, k_pe, q_norm, kv_norm, w_uq, w_ukv, cos, sin):
    b, s, _ = c_q.shape
    q = (_rms(c_q, q_norm) @ w_uq).reshape(b, s, MLA_HEADS, MLA_NOPE + MLA_ROPE)
    q = jnp.concatenate([q[..., :MLA_NOPE], _apply_rope(q[..., MLA_NOPE:], cos, sin)], axis=-1)
    kv = (_rms(c_kv, kv_norm) @ w_ukv).reshape(b, s, MLA_HEADS, MLA_NOPE + MLA_V)
    k_nope, v = kv[..., :MLA_NOPE], kv[..., MLA_NOPE:]
    k_pe = _apply_rope(k_pe, cos, sin)
    scale = (MLA_NOPE + MLA_ROPE) ** -0.5

    def attend(qblk):
        qn, qp = qblk[..., :MLA_NOPE], qblk[..., MLA_NOPE:]
        sc = jnp.einsum('bqhn,bkhn->bhqk', qn, k_nope) + jnp.einsum('bqhr,bkr->bhqk', qp, k_pe)
        p = jax.nn.softmax(sc.astype(jnp.float32) * scale, axis=-1).astype(v.dtype)
        return jnp.einsum('bhqk,bkhv->bqhv', p, v)

    return _sweep_query_blocks(attend, q).reshape(b, s, MLA_OUT)


def _diff_branch(q, k, v, lam, subln, lambda_init, cos, sin):
    b, s, _ = q.shape
    q = _apply_rope(q.reshape(b, s, DIFF_HEADS, 2, DIFF_D), cos, sin)
    k = _apply_rope(k.reshape(b, s, DIFF_HEADS, 2, DIFF_D), cos, sin)
    v = v.reshape(b, s, DIFF_HEADS, 2 * DIFF_D)
    lf = lam.astype(jnp.float32)
    lambda_full = jnp.exp(jnp.sum(lf[0] * lf[1])) - jnp.exp(jnp.sum(lf[2] * lf[3])) + lambda_init
    scale = DIFF_D ** -0.5

    def attend(qblk):
        sc = jnp.einsum('bqhcd,bkhcd->bhcqk', qblk, k).astype(jnp.float32) * scale
        p = jax.nn.softmax(sc, axis=-1)
        w = (p[:, :, 0] - lambda_full * p[:, :, 1]).astype(v.dtype)
        return jnp.einsum('bhqk,bkhe->bqhe', w, v)

    o = _sweep_query_blocks(attend, q)
    o = _rms(o, subln, 1e-5) * (1.0 - lambda_init)
    return o.reshape(b, s, DIFF_OUT)


def _mem_branch(q, mem, w_kv):
    b, s, _ = q.shape
    q = q.reshape(b, s, MEM_HEADS, MEM_HD)
    kv = (mem @ w_kv).reshape(b, mem.shape[1], 2, MEM_HEADS, MEM_HD)
    k, v = kv[:, :, 0], kv[:, :, 1]
    sc = jnp.einsum('bqhd,bmhd->bhqm', q, k).astype(jnp.float32) * (MEM_HD ** -0.5)
    p = jax.nn.softmax(sc, axis=-1).astype(v.dtype)
    return jnp.einsum('bhqm,bmhd->bqhd', p, v).reshape(b, s, MEM_OUT)


def setup_inputs(seed: int = 0) -> dict:
    key = jax.random.key(seed)
    ks = jax.random.split(key, 24)
    f32 = jnp.float32
    L, D = DEPTH, D_MODEL

    def w(k, shape, fan_in, gain=1.0):
        return jax.random.normal(k, shape, f32) * (gain * fan_in ** -0.5)

    def gain(k, shape):
        return 1.0 + 0.02 * jax.random.normal(k, shape, f32)

    def bias(k, shape):
        return 0.02 * jax.random.normal(k, shape, f32)

    x = jax.random.normal(ks[0], (BATCH, SEQ, D), f32)
    mem = jax.random.normal(ks[1], (BATCH, MEM_LEN, D), f32)
    offset = jax.random.randint(ks[2], (BATCH, 1), 0, SEQ, dtype=jnp.int32)
    positions = (jnp.arange(SEQ, dtype=jnp.int32)[None, :] + offset).astype(jnp.int32)
    return {
        'x': x,
        'mem': mem,
        'positions': positions,
        'w_in': w(ks[3], (L, D, N_IN), D),
        'b_gate': bias(ks[4], (L, N_BRANCH, D)),
        'mla_q_norm': gain(ks[5], (L, MLA_Q_LORA)),
        'mla_kv_norm': gain(ks[6], (L, MLA_KV_LORA)),
        'mla_w_uq': w(ks[7], (L, MLA_Q_LORA, MLA_HEADS * (MLA_NOPE + MLA_ROPE)), MLA_Q_LORA),
        'mla_w_ukv': w(ks[8], (L, MLA_KV_LORA, MLA_HEADS * (MLA_NOPE + MLA_V)), MLA_KV_LORA),
        'diff_lambda': 0.1 * jax.random.normal(ks[9], (L, 4, DIFF_D), f32),
        'diff_subln': gain(ks[10], (L, 2 * DIFF_D)),
        'mem_w_kv': w(ks[11], (L, D, 2 * MEM_OUT), D),
        'w_branch': w(ks[12], (L, BRANCH_W, D), BRANCH_W // N_BRANCH),
        'w_out': w(ks[13], (L, D, D), D, DEEPNORM_BETA),
        'ln1_g': gain(ks[14], (L, D)),
        'ln1_b': bias(ks[15], (L, D)),
        'mlp_w1': w(ks[16], (L, D, D_FF), D),
        'mlp_w2': w(ks[17], (L, D_FF, D), D_FF, DEEPNORM_BETA),
        'ln2_g': gain(ks[18], (L, D)),
        'ln2_b': bias(ks[19], (L, D)),
    }


def reference(x, mem, positions, w_in, b_gate, mla_q_norm, mla_kv_norm, mla_w_uq, mla_w_ukv,
              diff_lambda, diff_subln, mem_w_kv, w_branch, w_out, ln1_g, ln1_b,
              mlp_w1, mlp_w2, ln2_g, ln2_b):
    b, s, d = x.shape
    cos_m, sin_m = _rope_tables(positions, MLA_ROPE, x.dtype)
    cos_d, sin_d = _rope_tables(positions, DIFF_ROT, x.dtype)
    splits = _split_points()
    for l in range(DEPTH):
        lambda_init = 0.8 - 0.6 * math.exp(-0.3 * l)
        h = x @ w_in[l]
        c_q, c_kv, k_pe, dq, dk, dv, mq, gate_pre = jnp.split(h, splits, axis=-1)
        o_mla = _mla_branch(c_q, c_kv, k_pe, mla_q_norm[l], mla_kv_norm[l], mla_w_uq[l], mla_w_ukv[l], cos_m, sin_m)
        o_diff = _diff_branch(dq, dk, dv, diff_lambda[l], diff_subln[l], lambda_init, cos_d, sin_d)
        o_mem = _mem_branch(mq, mem, mem_w_kv[l])
        g = jax.nn.sigmoid(gate_pre.reshape(b, s, N_BRANCH, d) + b_gate[l])
        wb = w_branch[l]
        merged = (g[:, :, 0] * (o_mla @ wb[:MLA_OUT])
                  + g[:, :, 1] * (o_diff @ wb[MLA_OUT:MLA_OUT + DIFF_OUT])
                  + g[:, :, 2] * (o_mem @ wb[MLA_OUT + DIFF_OUT:]))
        x = _layernorm(DEEPNORM_ALPHA * x + merged @ w_out[l], ln1_g[l], ln1_b[l])
        f = jnp.square(jax.nn.relu(x @ mlp_w1[l])) @ mlp_w2[l]
        x = _layernorm(DEEPNORM_ALPHA * x + f, ln2_g[l], ln2_b[l])
    return x
```

```python
import functools
import math

import jax
import jax.numpy as jnp
from jax import lax
from jax.experimental import pallas as pl
from jax.experimental.pallas import tpu as pltpu

D_MODEL = 1024
DEPTH = 4
ROPE_THETA = 500000.0
MLA_HEADS = 8
MLA_NOPE = 64
MLA_ROPE = 32
MLA_V = 64
MLA_Q_LORA = 384
MLA_KV_LORA = 256
DIFF_HEADS = 8
DIFF_D = 64
DIFF_ROT = DIFF_D // 4
MEM_HEADS = 4
MEM_HD = 128
N_BRANCH = 3
D_FF = 4 * D_MODEL
DEEPNORM_ALPHA = (2 * DEPTH) ** 0.25
LN_EPS = 1e-5
MLA_OUT = MLA_HEADS * MLA_V
DIFF_QK = DIFF_HEADS * 2 * DIFF_D
DIFF_OUT = DIFF_HEADS * 2 * DIFF_D
MEM_OUT = MEM_HEADS * MEM_HD

LANE = 128
V7X_VMEM_BYTES = 64 * 1024 * 1024

MLA_ROPE_LO = MLA_NOPE
MLA_ROPE_HALF = MLA_ROPE // 2
DIFF_ROPE_HALF = DIFF_ROT // 2

BF16 = jnp.bfloat16
F32 = jnp.float32


def _dot(a, b):
    return jnp.dot(a, b, preferred_element_type=F32)


def _dot_nt(a, b):
    return lax.dot_general(a, b, (((1,), (1,)), ((), ())), preferred_element_type=F32)


def _compiler_params(semantics, vmem_bytes):
    limit = min(int(vmem_bytes), V7X_VMEM_BYTES - 8 * 1024 * 1024)
    return pltpu.CompilerParams(dimension_semantics=semantics, vmem_limit_bytes=limit)


def _nbytes(shape, dtype):
    return math.prod(shape) * jnp.dtype(dtype).itemsize


def _vmem_estimate(blocks, temps):
    return 2 * sum(_nbytes(s, d) for s, d in blocks) + sum(_nbytes(s, d) for s, d in temps) + (4 << 20)


def _rope_rotate(y, cos, sin_signed, x1_mask, half):
    partner = jnp.where(x1_mask, pltpu.roll(y, LANE - half, 1), pltpu.roll(y, half, 1))
    return y * cos + partner * sin_signed


def _x1_mask_mla(rows):
    lane = lax.broadcasted_iota(jnp.int32, (rows, LANE), 1)
    return (lane >= MLA_ROPE_LO) & (lane < MLA_ROPE_LO + MLA_ROPE_HALF)


def _x1_mask_diff(rows):
    lane = lax.broadcasted_iota(jnp.int32, (rows, LANE), 1) % DIFF_D
    return lane < DIFF_ROPE_HALF


def _layernorm(z, g, b):
    mu = jnp.mean(z, axis=-1, keepdims=True)
    zc = z - mu
    var = jnp.mean(zc * zc, axis=-1, keepdims=True)
    return zc * lax.rsqrt(var + LN_EPS) * g + b


def _rope_table_kernel(pos_ref, inv_ref, sgn_ref, cos_ref, sin_ref):
    ang = pos_ref[...].astype(F32) * inv_ref[...]
    cos_ref[...] = jnp.cos(ang)
    sin_ref[...] = jnp.sin(ang) * sgn_ref[...]


def _rope_tables(pos, inv_lane, sgn_lane, tm=2048):
    t = pos.shape[0]
    return pl.pallas_call(
        _rope_table_kernel,
        grid=(t // tm,),
        in_specs=[
            pl.BlockSpec((tm, 1), lambda i: (i, 0)),
            pl.BlockSpec((1, LANE), lambda i: (0, 0)),
            pl.BlockSpec((1, LANE), lambda i: (0, 0)),
        ],
        out_specs=[pl.BlockSpec((tm, LANE), lambda i: (i, 0))] * 2,
        out_shape=[jax.ShapeDtypeStruct((t, LANE), F32)] * 2,
        compiler_params=_compiler_params(("parallel",), 32 << 20),
        name="rope_tables",
    )(pos, inv_lane, sgn_lane)


def _mla_proj_kernel(x_ref, wa_ref, gq_ref, gkv_ref, wuq_ref, wuk_ref, wuv_ref, cos_ref, sin_ref,
                     q_ref, k_ref, v_ref, *, scale):
    tm = x_ref.shape[0]
    h = _dot(x_ref[...], wa_ref[...])
    cq = h[:, :MLA_Q_LORA]
    ckv = h[:, MLA_Q_LORA:MLA_Q_LORA + MLA_KV_LORA]
    kpe = h[:, MLA_Q_LORA + MLA_KV_LORA:]

    def rms(v, g):
        return (v * lax.rsqrt(jnp.mean(v * v, axis=-1, keepdims=True) + 1e-6) * g).astype(BF16)

    cqn = rms(cq, gq_ref[...])
    ckvn = rms(ckv, gkv_ref[...])
    cos = cos_ref[...]
    sin = sin_ref[...]
    x1 = _x1_mask_mla(tm)
    kpe_rot = _rope_rotate(kpe, cos, sin, x1, MLA_ROPE_HALF)

    q = _dot(cqn, wuq_ref[...])
    kn = _dot(ckvn, wuk_ref[...])
    for hd in range(MLA_HEADS):
        sl = slice(hd * LANE, (hd + 1) * LANE)
        qh = _rope_rotate(q[:, sl], cos, sin, x1, MLA_ROPE_HALF) * scale
        q_ref[:, sl] = qh.astype(BF16)
        k_ref[:, sl] = (kn[:, sl] + kpe_rot).astype(BF16)
    v_ref[...] = _dot(ckvn, wuv_ref[...]).astype(BF16)


def _mla_proj(xb, wa, gq, gkv, wuq, wuk, wuv, cos, sin, tm=512):
    t = xb.shape[0]
    na = wa.shape[1]
    nq = MLA_HEADS * LANE
    full = lambda shape: pl.BlockSpec(shape, lambda i: (0, 0))
    rows = lambda n: pl.BlockSpec((tm, n), lambda i: (i, 0))
    vmem = _vmem_estimate(
        [((tm, D_MODEL), BF16), (wa.shape, BF16), (wuq.shape, BF16), (wuk.shape, BF16), (wuv.shape, BF16),
         ((tm, LANE), F32), ((tm, LANE), F32), ((tm, nq), BF16), ((tm, nq), BF16), ((tm, MLA_OUT), BF16)],
        [((tm, na), F32), ((tm, nq), F32), ((tm, nq), F32), ((tm, nq), F32)])
    return pl.pallas_call(
        functools.partial(_mla_proj_kernel, scale=(MLA_NOPE + MLA_ROPE) ** -0.5),
        grid=(t // tm,),
        in_specs=[rows(D_MODEL), full(wa.shape), full(gq.shape), full(gkv.shape), full(wuq.shape),
                  full(wuk.shape), full(wuv.shape), rows(LANE), rows(LANE)],
        out_specs=[rows(nq), rows(nq), rows(MLA_OUT)],
        out_shape=[jax.ShapeDtypeStruct((t, nq), BF16), jax.ShapeDtypeStruct((t, nq), BF16),
                   jax.ShapeDtypeStruct((t, MLA_OUT), BF16)],
        compiler_params=_compiler_params(("parallel",), vmem),
        name="mla_proj",
    )(xb, wa, gq, gkv, wuq, wuk, wuv, cos, sin)


def _proj_kernel(*refs, rope, scale):
    if rope:
        x_ref, w_ref, cos_ref, sin_ref, o_ref = refs
    else:
        x_ref, w_ref, o_ref = refs
    tm = x_ref.shape[0]
    y = _dot(x_ref[...], w_ref[...])
    if not rope:
        if scale != 1.0:
            y = y * scale
        o_ref[...] = y.astype(o_ref.dtype)
        return
    cos = cos_ref[...]
    sin = sin_ref[...]
    x1 = _x1_mask_diff(tm)
    for hd in range(y.shape[1] // LANE):
        sl = slice(hd * LANE, (hd + 1) * LANE)
        yh = _rope_rotate(y[:, sl], cos, sin, x1, DIFF_ROPE_HALF)
        if scale != 1.0:
            yh = yh * scale
        o_ref[:, sl] = yh.astype(o_ref.dtype)


def _proj(xb, w, cos=None, sin=None, scale=1.0, tm=512, name="proj"):
    t, kdim = xb.shape
    n = w.shape[1]
    rope = cos is not None
    in_specs = [pl.BlockSpec((tm, kdim), lambda i: (i, 0)), pl.BlockSpec((kdim, n), lambda i: (0, 0))]
    args = [xb, w]
    blocks = [((tm, kdim), BF16), ((kdim, n), BF16), ((tm, n), BF16)]
    if rope:
        in_specs += [pl.BlockSpec((tm, LANE), lambda i: (i, 0))] * 2
        args += [cos, sin]
        blocks += [((tm, LANE), F32)] * 2
    vmem = _vmem_estimate(blocks, [((tm, n), F32), ((tm, n), F32)])
    return pl.pallas_call(
        functools.partial(_proj_kernel, rope=rope, scale=scale),
        grid=(t // tm,),
        in_specs=in_specs,
        out_specs=pl.BlockSpec((tm, n), lambda i: (i, 0)),
        out_shape=jax.ShapeDtypeStruct((t, n), BF16),
        compiler_params=_compiler_params(("parallel",), vmem),
        name=name,
    )(*args)


def _mla_attn_kernel(q_ref, k_ref, v_ref, o_ref, *, group):
    tq = q_ref.shape[0]
    v = v_ref[...]
    vlane = lax.broadcasted_iota(jnp.int32, (tq, group * MLA_V), 1) // MLA_V
    out = jnp.zeros((tq, group * MLA_V), F32)
    for j in range(group):
        sl = slice(j * LANE, (j + 1) * LANE)
        s = _dot_nt(q_ref[:, sl], k_ref[:, sl])
        p = jnp.exp(s - jnp.max(s, axis=-1, keepdims=True))
        inv_l = 1.0 / jnp.sum(p, axis=-1, keepdims=True)
        pv = _dot(p.astype(BF16), v) * inv_l
        out = jnp.where(vlane == j, pv, out)
    o_ref[...] = out.astype(o_ref.dtype)


def _mla_attn(q, k, v, batch, seq, tq=256, group=2):
    t = q.shape[0]
    nq = seq // tq
    grid = (batch, MLA_HEADS // group, nq)
    vmem = _vmem_estimate(
        [((tq, group * LANE), BF16), ((seq, group * LANE), BF16), ((seq, group * MLA_V), BF16),
         ((tq, group * MLA_V), BF16)],
        [((tq, seq), F32)] * 3 + [((tq, seq), BF16)])
    return pl.pallas_call(
        functools.partial(_mla_attn_kernel, group=group),
        grid=grid,
        in_specs=[
            pl.BlockSpec((tq, group * LANE), lambda b, g, i: (b * nq + i, g)),
            pl.BlockSpec((seq, group * LANE), lambda b, g, i: (b, g)),
            pl.BlockSpec((seq, group * MLA_V), lambda b, g, i: (b, g)),
        ],
        out_specs=pl.BlockSpec((tq, group * MLA_V), lambda b, g, i: (b * nq + i, g)),
        out_shape=jax.ShapeDtypeStruct((t, MLA_OUT), BF16),
        compiler_params=_compiler_params(("parallel", "parallel", "arbitrary"), vmem),
        name="mla_attn",
    )(q, k, v)


def _diff_attn_kernel(q_ref, k_ref, v_ref, lam_ref, g_ref, o_ref, *, lambda_init):
    tq = q_ref.shape[0]
    lam = lam_ref[...]
    lam_a = jnp.sum(lam[0:1] * lam[1:2], axis=-1, keepdims=True)
    lam_b = jnp.sum(lam[2:3] * lam[3:4], axis=-1, keepdims=True)
    lambda_full = jnp.exp(lam_a) - jnp.exp(lam_b) + lambda_init

    q = q_ref[...]
    k = k_ref[...]
    map0 = lax.broadcasted_iota(jnp.int32, (tq, LANE), 1) < DIFF_D
    zero = jnp.zeros_like(q)
    s0 = _dot_nt(jnp.where(map0, q, zero), k)
    s1 = _dot_nt(jnp.where(map0, zero, q), k)
    p0 = jnp.exp(s0 - jnp.max(s0, axis=-1, keepdims=True))
    p1 = jnp.exp(s1 - jnp.max(s1, axis=-1, keepdims=True))
    c0 = 1.0 / jnp.sum(p0, axis=-1, keepdims=True)
    c1 = lambda_full / jnp.sum(p1, axis=-1, keepdims=True)
    w = (p0 * c0 - p1 * c1).astype(BF16)
    o = _dot(w, v_ref[...])
    o = o * lax.rsqrt(jnp.mean(o * o, axis=-1, keepdims=True) + 1e-5) * g_ref[...]
    o_ref[...] = (o * (1.0 - lambda_init)).astype(o_ref.dtype)


def _diff_attn(q, k, v, lam, subln, lambda_init, batch, seq, tq=256):
    t = q.shape[0]
    nq = seq // tq
    vmem = _vmem_estimate(
        [((tq, LANE), BF16), ((seq, LANE), BF16), ((seq, LANE), BF16), ((tq, LANE), BF16)],
        [((tq, seq), F32)] * 5 + [((tq, seq), BF16)])
    return pl.pallas_call(
        functools.partial(_diff_attn_kernel, lambda_init=lambda_init),
        grid=(batch, DIFF_HEADS, nq),
        in_specs=[
            pl.BlockSpec((tq, LANE), lambda b, h, i: (b * nq + i, h)),
            pl.BlockSpec((seq, LANE), lambda b, h, i: (b, h)),
            pl.BlockSpec((seq, LANE), lambda b, h, i: (b, h)),
            pl.BlockSpec(lam.shape, lambda b, h, i: (0, 0)),
            pl.BlockSpec(subln.shape, lambda b, h, i: (0, 0)),
        ],
        out_specs=pl.BlockSpec((tq, LANE), lambda b, h, i: (b * nq + i, h)),
        out_shape=jax.ShapeDtypeStruct((t, DIFF_OUT), BF16),
        compiler_params=_compiler_params(("parallel", "parallel", "arbitrary"), vmem),
        name="diff_attn",
    )(q, k, v, lam, subln)


def _mem_attn_kernel(q_ref, kv_ref, o_ref):
    for hd in range(MEM_HEADS):
        sl = slice(hd * MEM_HD, (hd + 1) * MEM_HD)
        vsl = slice(MEM_OUT + hd * MEM_HD, MEM_OUT + (hd + 1) * MEM_HD)
        s = _dot_nt(q_ref[:, sl], kv_ref[:, sl])
        p = jnp.exp(s - jnp.max(s, axis=-1, keepdims=True))
        inv_l = 1.0 / jnp.sum(p, axis=-1, keepdims=True)
        o_ref[:, sl] = (_dot(p.astype(BF16), kv_ref[:, vsl]) * inv_l).astype(o_ref.dtype)


def _mem_attn(q, kv, batch, seq, mem_len, tq=1024):
    t = q.shape[0]
    nq = seq // tq
    vmem = _vmem_estimate(
        [((tq, MEM_OUT), BF16), ((mem_len, 2 * MEM_OUT), BF16), ((tq, MEM_OUT), BF16)],
        [((tq, mem_len), F32)] * 3 + [((tq, MEM_OUT), F32)])
    return pl.pallas_call(
        _mem_attn_kernel,
        grid=(batch, nq),
        in_specs=[
            pl.BlockSpec((tq, MEM_OUT), lambda b, i: (b * nq + i, 0)),
            pl.BlockSpec((mem_len, 2 * MEM_OUT), lambda b, i: (b, 0)),
        ],
        out_specs=pl.BlockSpec((tq, MEM_OUT), lambda b, i: (b * nq + i, 0)),
        out_shape=jax.ShapeDtypeStruct((t, MEM_OUT), BF16),
        compiler_params=_compiler_params(("parallel", "arbitrary"), vmem),
        name="mem_attn",
    )(q, kv)


def _merge_kernel(x_ref, xb_ref, omla_ref, odiff_ref, omem_ref, wg_ref, bg_ref, wbm_ref, wbd_ref, wbe_ref,
                  wo_ref, g_ref, b_ref, y_ref, yb_ref):
    xb = xb_ref[...]
    merged = None
    for i, (o_ref, wb_ref) in enumerate(((omla_ref, wbm_ref), (odiff_ref, wbd_ref), (omem_ref, wbe_ref))):
        sl = slice(i * D_MODEL, (i + 1) * D_MODEL)
        gate = jax.nn.sigmoid(_dot(xb, wg_ref[:, sl]) + bg_ref[i:i + 1, :])
        term = gate * _dot(o_ref[...], wb_ref[...])
        merged = term if merged is None else merged + term
    z = DEEPNORM_ALPHA * x_ref[...] + _dot(merged.astype(BF16), wo_ref[...])
    y = _layernorm(z, g_ref[...], b_ref[...])
    y_ref[...] = y
    yb_ref[...] = y.astype(BF16)


def _merge(x, xb, omla, odiff, omem, wg, bg, wbm, wbd, wbe, wo, g, b, tm=512):
    t = x.shape[0]
    full = lambda a: pl.BlockSpec(a.shape, lambda i: (0, 0))
    rows = lambda n: pl.BlockSpec((tm, n), lambda i: (i, 0))
    vmem = _vmem_estimate(
        [((tm, D_MODEL), F32), ((tm, D_MODEL), BF16), ((tm, MLA_OUT), BF16), ((tm, DIFF_OUT), BF16),
         ((tm, MEM_OUT), BF16), (wg.shape, BF16), (wbm.shape, BF16), (wbd.shape, BF16), (wbe.shape, BF16),
         (wo.shape, BF16), ((tm, D_MODEL), F32), ((tm, D_MODEL), BF16)],
        [((tm, D_MODEL), F32)] * 5)
    return pl.pallas_call(
        _merge_kernel,
        grid=(t // tm,),
        in_specs=[rows(D_MODEL), rows(D_MODEL), rows(MLA_OUT), rows(DIFF_OUT), rows(MEM_OUT),
                  full(wg), full(bg), full(wbm), full(wbd), full(wbe), full(wo), full(g), full(b)],
        out_specs=[rows(D_MODEL), rows(D_MODEL)],
        out_shape=[jax.ShapeDtypeStruct((t, D_MODEL), F32), jax.ShapeDtypeStruct((t, D_MODEL), BF16)],
        compiler_params=_compiler_params(("parallel",), vmem),
        name="merge",
    )(x, xb, omla, odiff, omem, wg, bg, wbm, wbd, wbe, wo, g, b)


def _mlp_kernel(x_ref, xb_ref, w1_ref, w2_ref, g_ref, b_ref, y_ref, yb_ref, acc_ref):
    kf = pl.program_id(1)

    @pl.when(kf == 0)
    def _():
        acc_ref[...] = jnp.zeros_like(acc_ref)

    h = jnp.maximum(_dot(xb_ref[...], w1_ref[...]), 0.0)
    acc_ref[...] += _dot((h * h).astype(BF16), w2_ref[...])

    @pl.when(kf == pl.num_programs(1) - 1)
    def _():
        y = _layernorm(DEEPNORM_ALPHA * x_ref[...] + acc_ref[...], g_ref[...], b_ref[...])
        y_ref[...] = y
        yb_ref[...] = y.astype(BF16)


def _mlp(x, xb, w1, w2, g, b, tm=1024, tf=1024):
    t = x.shape[0]
    vmem = _vmem_estimate(
        [((tm, D_MODEL), F32), ((tm, D_MODEL), BF16), ((D_MODEL, tf), BF16), ((tf, D_MODEL), BF16),
         ((tm, D_MODEL), F32), ((tm, D_MODEL), BF16)],
        [((tm, D_MODEL), F32), ((tm, tf), F32), ((tm, tf), F32), ((tm, tf), BF16), ((tm, D_MODEL), F32)])
    return pl.pallas_call(
        _mlp_kernel,
        grid=(t // tm, D_FF // tf),
        in_specs=[
            pl.BlockSpec((tm, D_MODEL), lambda i, k: (i, 0)),
            pl.BlockSpec((tm, D_MODEL), lambda i, k: (i, 0)),
            pl.BlockSpec((D_MODEL, tf), lambda i, k: (0, k)),
            pl.BlockSpec((tf, D_MODEL), lambda i, k: (k, 0)),
            pl.BlockSpec((1, D_MODEL), lambda i, k: (0, 0)),
            pl.BlockSpec((1, D_MODEL), lambda i, k: (0, 0)),
        ],
        out_specs=[pl.BlockSpec((tm, D_MODEL), lambda i, k: (i, 0))] * 2,
        out_shape=[jax.ShapeDtypeStruct((t, D_MODEL), F32), jax.ShapeDtypeStruct((t, D_MODEL), BF16)],
        scratch_shapes=[pltpu.VMEM((tm, D_MODEL), F32)],
        compiler_params=_compiler_params(("parallel", "arbitrary"), vmem),
        name="mlp",
    )(x, xb, w1, w2, g, b)


def _lane_patterns():
    lane = jnp.arange(LANE)
    inv_m = ROPE_THETA ** (-jnp.arange(0, MLA_ROPE, 2, dtype=F32) / MLA_ROPE)
    rel = lane - MLA_ROPE_LO
    in_rope = (rel >= 0) & (rel < MLA_ROPE)
    inv_mla = jnp.where(in_rope, inv_m[jnp.clip(rel, 0, MLA_ROPE - 1) % MLA_ROPE_HALF], 0.0)
    sgn_mla = jnp.where(in_rope, jnp.where(rel < MLA_ROPE_HALF, -1.0, 1.0), 0.0)
    inv_d = ROPE_THETA ** (-jnp.arange(0, DIFF_ROT, 2, dtype=F32) / DIFF_ROT)
    rel = lane % DIFF_D
    in_rope = rel < DIFF_ROT
    inv_diff = jnp.where(in_rope, inv_d[rel % DIFF_ROPE_HALF], 0.0)
    sgn_diff = jnp.where(in_rope, jnp.where(rel < DIFF_ROPE_HALF, -1.0, 1.0), 0.0)
    row = lambda a: a.astype(F32).reshape(1, LANE)
    return row(inv_mla), row(sgn_mla), row(inv_diff), row(sgn_diff)


def _layer_weights(w_in, mla_w_uq, mla_w_ukv, w_branch):
    d = w_in.shape[0]
    o = 0
    w_cq = w_in[:, o:o + MLA_Q_LORA]; o += MLA_Q_LORA
    w_ckv = w_in[:, o:o + MLA_KV_LORA]; o += MLA_KV_LORA
    w_kpe = w_in[:, o:o + MLA_ROPE]; o += MLA_ROPE
    w_dq = w_in[:, o:o + DIFF_QK]; o += DIFF_QK
    w_dk = w_in[:, o:o + DIFF_QK]; o += DIFF_QK
    w_dv = w_in[:, o:o + DIFF_OUT]; o += DIFF_OUT
    w_mq = w_in[:, o:o + MEM_OUT]; o += MEM_OUT
    w_gate = w_in[:, o:]
    w_a = jnp.concatenate(
        [w_cq, w_ckv, jnp.zeros((d, MLA_ROPE_LO), F32), w_kpe,
         jnp.zeros((d, LANE - MLA_ROPE_LO - MLA_ROPE), F32)], axis=1)
    qd = MLA_NOPE + MLA_ROPE
    w_uq = jnp.pad(mla_w_uq.reshape(MLA_Q_LORA, MLA_HEADS, qd), ((0, 0), (0, 0), (0, LANE - qd)))
    w_uq = w_uq.reshape(MLA_Q_LORA, MLA_HEADS * LANE)
    ukv = mla_w_ukv.reshape(MLA_KV_LORA, MLA_HEADS, MLA_NOPE + MLA_V)
    w_uk = jnp.pad(ukv[:, :, :MLA_NOPE], ((0, 0), (0, 0), (0, LANE - MLA_NOPE))).reshape(MLA_KV_LORA, MLA_HEADS * LANE)
    w_uv = ukv[:, :, MLA_NOPE:].reshape(MLA_KV_LORA, MLA_OUT)
    wb_mla = w_branch[:MLA_OUT]
    wb_diff = w_branch[MLA_OUT:MLA_OUT + DIFF_OUT]
    wb_mem = w_branch[MLA_OUT + DIFF_OUT:]
    cast = lambda a: a.astype(BF16)
    return tuple(map(cast, (w_a, w_uq, w_uk, w_uv, w_dq, w_dk, w_dv, w_mq, w_gate, wb_mla, wb_diff, wb_mem)))


def kernel(x, mem, positions, w_in, b_gate, mla_q_norm, mla_kv_norm, mla_w_uq, mla_w_ukv, diff_lambda, diff_subln,
           mem_w_kv, w_branch, w_out, ln1_g, ln1_b, mlp_w1, mlp_w2, ln2_g, ln2_b):
    batch, seq, d = x.shape
    mem_len = mem.shape[1]
    t = batch * seq
    depth = w_in.shape[0]

    inv_mla, sgn_mla, inv_diff, sgn_diff = _lane_patterns()
    pos = positions.reshape(t, 1)
    cos_m, sin_m = _rope_tables(pos, inv_mla, sgn_mla)
    cos_d, sin_d = _rope_tables(pos, inv_diff, sgn_diff)

    xf = x.reshape(t, d)
    xb = xf.astype(BF16)
    memb = mem.reshape(batch * mem_len, d).astype(BF16)
    row = lambda a: a.reshape(1, -1)

    for l in range(depth):
        lambda_init = 0.8 - 0.6 * math.exp(-0.3 * l)
        (w_a, w_uq, w_uk, w_uv, w_dq, w_dk, w_dv, w_mq, w_gate, wb_mla, wb_diff, wb_mem) = _layer_weights(
            w_in[l], mla_w_uq[l], mla_w_ukv[l], w_branch[l])

        q_m, k_m, v_m = _mla_proj(xb, w_a, row(mla_q_norm[l]), row(mla_kv_norm[l]), w_uq, w_uk, w_uv, cos_m, sin_m)
        o_mla = _mla_attn(q_m, k_m, v_m, batch, seq)

        q_d = _proj(xb, w_dq, cos_d, sin_d, scale=DIFF_D ** -0.5, name="diff_q_proj")
        k_d = _proj(xb, w_dk, cos_d, sin_d, name="diff_k_proj")
        v_d = _proj(xb, w_dv, name="diff_v_proj")
        o_diff = _diff_attn(q_d, k_d, v_d, diff_lambda[l], row(diff_subln[l]), lambda_init, batch, seq)

        q_e = _proj(xb, w_mq, scale=MEM_HD ** -0.5, name="mem_q_proj")
        kv_e = _proj(memb, mem_w_kv[l].astype(BF16), tm=mem_len, name="mem_kv_proj")
        o_mem = _mem_attn(q_e, kv_e, batch, seq, mem_len)

        xf, xb = _merge(xf, xb, o_mla, o_diff, o_mem, w_gate, b_gate[l], wb_mla, wb_diff, wb_mem,
                        w_out[l].astype(BF16), row(ln1_g[l]), row(ln1_b[l]))
        xf, xb = _mlp(xf, xb, mlp_w1[l].astype(BF16), mlp_w2[l].astype(BF16), row(ln2_g[l]), row(ln2_b[l]))
    return xf.reshape(batch, seq, d)
```

```python
import functools
import math

import jax
import jax.numpy as jnp
from jax import lax
from jax.experimental import pallas as pl
from jax.experimental.pallas import tpu as pltpu

D_MODEL = 1024
DEPTH = 4
ROPE_THETA = 500000.0
MLA_HEADS = 8
MLA_NOPE = 64
MLA_ROPE = 32
MLA_V = 64
MLA_Q_LORA = 384
MLA_KV_LORA = 256
DIFF_HEADS = 8
DIFF_D = 64
DIFF_ROT = DIFF_D // 4
MEM_HEADS = 4
MEM_HD = 128
N_BRANCH = 3
D_FF = 4 * D_MODEL
DEEPNORM_ALPHA = (2 * DEPTH) ** 0.25
LN_EPS = 1e-5
MLA_OUT = MLA_HEADS * MLA_V
DIFF_QK = DIFF_HEADS * 2 * DIFF_D
DIFF_OUT = DIFF_HEADS * 2 * DIFF_D
MEM_OUT = MEM_HEADS * MEM_HD

LANE = 128
V7X_VMEM_BYTES = 64 * 1024 * 1024

MLA_ROPE_LO = MLA_NOPE
MLA_ROPE_HALF = MLA_ROPE // 2
DIFF_ROPE_HALF = DIFF_ROT // 2

BF16 = jnp.bfloat16
F32 = jnp.float32
LOG2E = math.log2(math.e)


def _dot(a, b):
    return jnp.dot(a, b, preferred_element_type=F32)


def _dot_nt(a, b):
    return lax.dot_general(a, b, (((1,), (1,)), ((), ())), preferred_element_type=F32)


def _compiler_params(semantics, vmem_bytes):
    limit = min(int(vmem_bytes), V7X_VMEM_BYTES - 8 * 1024 * 1024)
    return pltpu.CompilerParams(dimension_semantics=semantics, vmem_limit_bytes=limit)


def _nbytes(shape, dtype):
    return math.prod(shape) * jnp.dtype(dtype).itemsize


def _vmem_estimate(blocks, temps):
    return 2 * sum(_nbytes(s, d) for s, d in blocks) + sum(_nbytes(s, d) for s, d in temps) + (4 << 20)


def _rope_rotate(y, cos, sin_signed, x1_mask, half):
    partner = jnp.where(x1_mask, pltpu.roll(y, LANE - half, 1), pltpu.roll(y, half, 1))
    return y * cos + partner * sin_signed


def _x1_mask_mla(rows):
    lane = lax.broadcasted_iota(jnp.int32, (rows, LANE), 1)
    return (lane >= MLA_ROPE_LO) & (lane < MLA_ROPE_LO + MLA_ROPE_HALF)


def _x1_mask_diff(rows):
    lane = lax.broadcasted_iota(jnp.int32, (rows, LANE), 1) % DIFF_D
    return lane < DIFF_ROPE_HALF


def _layernorm(z, g, b):
    mu = jnp.mean(z, axis=-1, keepdims=True)
    zc = z - mu
    var = jnp.mean(zc * zc, axis=-1, keepdims=True)
    return zc * lax.rsqrt(var + LN_EPS) * g + b


def _rope_table_kernel(pos_ref, inv_ref, sgn_ref, cos_ref, sin_ref):
    ang = pos_ref[...].astype(F32) * inv_ref[...]
    cos_ref[...] = jnp.cos(ang)
    sin_ref[...] = jnp.sin(ang) * sgn_ref[...]


def _rope_tables(pos, inv_lane, sgn_lane, tm=2048):
    t = pos.shape[0]
    return pl.pallas_call(
        _rope_table_kernel,
        grid=(t // tm,),
        in_specs=[
            pl.BlockSpec((tm, 1), lambda i: (i, 0)),
            pl.BlockSpec((1, LANE), lambda i: (0, 0)),
            pl.BlockSpec((1, LANE), lambda i: (0, 0)),
        ],
        out_specs=[pl.BlockSpec((tm, LANE), lambda i: (i, 0))] * 2,
        out_shape=[jax.ShapeDtypeStruct((t, LANE), F32)] * 2,
        compiler_params=_compiler_params(("parallel",), 32 << 20),
        name="rope_tables",
    )(pos, inv_lane, sgn_lane)


def _mla_proj_kernel(x_ref, wa_ref, gq_ref, gkv_ref, wuq_ref, wuk_ref, wuv_ref, cos_ref, sin_ref,
                     q_ref, k_ref, v_ref, *, scale):
    tm = x_ref.shape[0]
    h = _dot(x_ref[...], wa_ref[...])
    cq = h[:, :MLA_Q_LORA]
    ckv = h[:, MLA_Q_LORA:MLA_Q_LORA + MLA_KV_LORA]
    kpe = h[:, MLA_Q_LORA + MLA_KV_LORA:]

    def rms(v, g):
        return (v * lax.rsqrt(jnp.mean(v * v, axis=-1, keepdims=True) + 1e-6) * g).astype(BF16)

    cqn = rms(cq, gq_ref[...])
    ckvn = rms(ckv, gkv_ref[...])
    cos = cos_ref[...]
    sin = sin_ref[...]
    x1 = _x1_mask_mla(tm)
    kpe_rot = _rope_rotate(kpe, cos, sin, x1, MLA_ROPE_HALF)

    q = _dot(cqn, wuq_ref[...])
    kn = _dot(ckvn, wuk_ref[...])
    for hd in range(MLA_HEADS):
        sl = slice(hd * LANE, (hd + 1) * LANE)
        qh = _rope_rotate(q[:, sl], cos, sin, x1, MLA_ROPE_HALF) * scale
        q_ref[:, sl] = qh.astype(BF16)
        k_ref[:, sl] = (kn[:, sl] + kpe_rot).astype(BF16)
    v = _dot(ckvn, wuv_ref[...]).astype(BF16)
    ones = jnp.ones((tm, LANE), BF16)
    for pair in range(MLA_HEADS // 2):
        v_ref[:, 2 * pair * LANE:(2 * pair + 1) * LANE] = v[:, pair * LANE:(pair + 1) * LANE]
        v_ref[:, (2 * pair + 1) * LANE:(2 * pair + 2) * LANE] = ones


def _mla_proj(xb, wa, gq, gkv, wuq, wuk, wuv, cos, sin, tm=512):
    t = xb.shape[0]
    na = wa.shape[1]
    nq = MLA_HEADS * LANE
    nv = MLA_HEADS * LANE
    full = lambda shape: pl.BlockSpec(shape, lambda i: (0, 0))
    rows = lambda n: pl.BlockSpec((tm, n), lambda i: (i, 0))
    vmem = _vmem_estimate(
        [((tm, D_MODEL), BF16), (wa.shape, BF16), (wuq.shape, BF16), (wuk.shape, BF16), (wuv.shape, BF16),
         ((tm, LANE), F32), ((tm, LANE), F32), ((tm, nq), BF16), ((tm, nq), BF16), ((tm, nv), BF16)],
        [((tm, na), F32), ((tm, nq), F32), ((tm, nq), F32), ((tm, nq), F32)])
    return pl.pallas_call(
        functools.partial(_mla_proj_kernel, scale=LOG2E * (MLA_NOPE + MLA_ROPE) ** -0.5),
        grid=(t // tm,),
        in_specs=[rows(D_MODEL), full(wa.shape), full(gq.shape), full(gkv.shape), full(wuq.shape),
                  full(wuk.shape), full(wuv.shape), rows(LANE), rows(LANE)],
        out_specs=[rows(nq), rows(nq), rows(nv)],
        out_shape=[jax.ShapeDtypeStruct((t, nq), BF16), jax.ShapeDtypeStruct((t, nq), BF16),
                   jax.ShapeDtypeStruct((t, nv), BF16)],
        compiler_params=_compiler_params(("parallel",), vmem),
        name="mla_proj",
    )(xb, wa, gq, gkv, wuq, wuk, wuv, cos, sin)


def _proj_kernel(*refs, rope, scale, ones_ext):
    if rope:
        x_ref, w_ref, cos_ref, sin_ref, o_ref = refs
    else:
        x_ref, w_ref, o_ref = refs
    tm = x_ref.shape[0]
    y = _dot(x_ref[...], w_ref[...])
    if ones_ext:
        ones = jnp.ones((tm, LANE), o_ref.dtype)
        for hd in range(y.shape[1] // LANE):
            o_ref[:, 2 * hd * LANE:(2 * hd + 1) * LANE] = y[:, hd * LANE:(hd + 1) * LANE].astype(o_ref.dtype)
            o_ref[:, (2 * hd + 1) * LANE:(2 * hd + 2) * LANE] = ones
        return
    if not rope:
        if scale != 1.0:
            y = y * scale
        o_ref[...] = y.astype(o_ref.dtype)
        return
    cos = cos_ref[...]
    sin = sin_ref[...]
    x1 = _x1_mask_diff(tm)
    for hd in range(y.shape[1] // LANE):
        sl = slice(hd * LANE, (hd + 1) * LANE)
        yh = _rope_rotate(y[:, sl], cos, sin, x1, DIFF_ROPE_HALF)
        if scale != 1.0:
            yh = yh * scale
        o_ref[:, sl] = yh.astype(o_ref.dtype)


def _proj(xb, w, cos=None, sin=None, scale=1.0, ones_ext=False, tm=512, name="proj"):
    t, kdim = xb.shape
    n = w.shape[1]
    n_out = 2 * n if ones_ext else n
    rope = cos is not None
    in_specs = [pl.BlockSpec((tm, kdim), lambda i: (i, 0)), pl.BlockSpec((kdim, n), lambda i: (0, 0))]
    args = [xb, w]
    blocks = [((tm, kdim), BF16), ((kdim, n), BF16), ((tm, n_out), BF16)]
    if rope:
        in_specs += [pl.BlockSpec((tm, LANE), lambda i: (i, 0))] * 2
        args += [cos, sin]
        blocks += [((tm, LANE), F32)] * 2
    vmem = _vmem_estimate(blocks, [((tm, n), F32), ((tm, n), F32)])
    return pl.pallas_call(
        functools.partial(_proj_kernel, rope=rope, scale=scale, ones_ext=ones_ext),
        grid=(t // tm,),
        in_specs=in_specs,
        out_specs=pl.BlockSpec((tm, n_out), lambda i: (i, 0)),
        out_shape=jax.ShapeDtypeStruct((t, n_out), BF16),
        compiler_params=_compiler_params(("parallel",), vmem),
        name=name,
    )(*args)


def _softmax_pv(q, k, v_ext):
    s = _dot_nt(q, k)
    m = jnp.max(s, axis=-1, keepdims=True)
    p = jnp.exp2((s - m).astype(BF16))
    pv = _dot(p, v_ext)
    return pv[:, :LANE] / pv[:, LANE:]


def _mla_attn_kernel(q_ref, k_ref, v_ref, o_ref, *, pairs):
    tq = q_ref.shape[0]
    first = lax.broadcasted_iota(jnp.int32, (tq, LANE), 1) < MLA_V
    for pr in range(pairs):
        v_ext = v_ref[:, 2 * pr * LANE:(2 * pr + 2) * LANE]
        outs = []
        for j in range(2):
            sl = slice((2 * pr + j) * LANE, (2 * pr + j + 1) * LANE)
            outs.append(_softmax_pv(q_ref[:, sl], k_ref[:, sl], v_ext))
        o_ref[:, pr * LANE:(pr + 1) * LANE] = jnp.where(first, outs[0], outs[1]).astype(o_ref.dtype)


def _mla_attn(q, k, v, batch, seq, tq=512, pairs=2):
    t = q.shape[0]
    nq = seq // tq
    wide = 2 * pairs * LANE
    grid = (batch, MLA_HEADS // (2 * pairs), nq)
    vmem = _vmem_estimate(
        [((tq, wide), BF16), ((seq, wide), BF16), ((seq, wide), BF16), ((tq, pairs * LANE), BF16)],
        ([((tq, seq), F32)] * 2 + [((tq, seq), BF16)] * 2) * 2 * pairs)
    return pl.pallas_call(
        functools.partial(_mla_attn_kernel, pairs=pairs),
        grid=grid,
        in_specs=[
            pl.BlockSpec((tq, wide), lambda b, g, i: (b * nq + i, g)),
            pl.BlockSpec((seq, wide), lambda b, g, i: (b, g)),
            pl.BlockSpec((seq, wide), lambda b, g, i: (b, g)),
        ],
        out_specs=pl.BlockSpec((tq, pairs * LANE), lambda b, g, i: (b * nq + i, g)),
        out_shape=jax.ShapeDtypeStruct((t, MLA_OUT), BF16),
        compiler_params=_compiler_params(("parallel", "parallel", "arbitrary"), vmem),
        name="mla_attn",
    )(q, k, v)


def _diff_attn_kernel(q_ref, k_ref, v_ref, lam_ref, g_ref, o_ref, *, lambda_init, heads):
    tq = q_ref.shape[0]
    lam = lam_ref[...]
    lam_a = jnp.sum(lam[0:1] * lam[1:2], axis=-1, keepdims=True)
    lam_b = jnp.sum(lam[2:3] * lam[3:4], axis=-1, keepdims=True)
    lambda_full = jnp.exp(lam_a) - jnp.exp(lam_b) + lambda_init
    map0 = lax.broadcasted_iota(jnp.int32, (tq, LANE), 1) < DIFF_D
    gain = g_ref[...] * (1.0 - lambda_init)
    for hd in range(heads):
        sl = slice(hd * LANE, (hd + 1) * LANE)
        q = q_ref[:, sl]
        k = k_ref[:, sl]
        v_ext = v_ref[:, 2 * hd * LANE:(2 * hd + 2) * LANE]
        zero = jnp.zeros_like(q)
        o = _softmax_pv(jnp.where(map0, q, zero), k, v_ext) - lambda_full * _softmax_pv(
            jnp.where(map0, zero, q), k, v_ext)
        o = o * lax.rsqrt(jnp.mean(o * o, axis=-1, keepdims=True) + 1e-5) * gain
        o_ref[:, sl] = o.astype(o_ref.dtype)


def _diff_attn(q, k, v, lam, subln, lambda_init, batch, seq, tq=256, heads=4):
    t = q.shape[0]
    nq = seq // tq
    wide = heads * LANE
    vmem = _vmem_estimate(
        [((tq, wide), BF16), ((seq, wide), BF16), ((seq, 2 * wide), BF16), ((tq, wide), BF16)],
        ([((tq, seq), F32)] * 2 + [((tq, seq), BF16)] * 2) * 2 * heads)
    return pl.pallas_call(
        functools.partial(_diff_attn_kernel, lambda_init=lambda_init, heads=heads),
        grid=(batch, DIFF_HEADS // heads, nq),
        in_specs=[
            pl.BlockSpec((tq, wide), lambda b, h, i: (b * nq + i, h)),
            pl.BlockSpec((seq, wide), lambda b, h, i: (b, h)),
            pl.BlockSpec((seq, 2 * wide), lambda b, h, i: (b, h)),
            pl.BlockSpec(lam.shape, lambda b, h, i: (0, 0)),
            pl.BlockSpec(subln.shape, lambda b, h, i: (0, 0)),
        ],
        out_specs=pl.BlockSpec((tq, wide), lambda b, h, i: (b * nq + i, h)),
        out_shape=jax.ShapeDtypeStruct((t, DIFF_OUT), BF16),
        compiler_params=_compiler_params(("parallel", "parallel", "arbitrary"), vmem),
        name="diff_attn",
    )(q, k, v, lam, subln)


def _mem_attn_kernel(q_ref, kv_ref, o_ref):
    for hd in range(MEM_HEADS):
        sl = slice(hd * MEM_HD, (hd + 1) * MEM_HD)
        vsl = slice(MEM_OUT + hd * MEM_HD, MEM_OUT + (hd + 1) * MEM_HD)
        s = _dot_nt(q_ref[:, sl], kv_ref[:, sl])
        p = jnp.exp(s - jnp.max(s, axis=-1, keepdims=True))
        inv_l = 1.0 / jnp.sum(p, axis=-1, keepdims=True)
        o_ref[:, sl] = (_dot(p.astype(BF16), kv_ref[:, vsl]) * inv_l).astype(o_ref.dtype)


def _mem_attn(q, kv, batch, seq, mem_len, tq=1024):
    t = q.shape[0]
    nq = seq // tq
    vmem = _vmem_estimate(
        [((tq, MEM_OUT), BF16), ((mem_len, 2 * MEM_OUT), BF16), ((tq, MEM_OUT), BF16)],
        [((tq, mem_len), F32)] * 3 + [((tq, MEM_OUT), F32)])
    return pl.pallas_call(
        _mem_attn_kernel,
        grid=(batch, nq),
        in_specs=[
            pl.BlockSpec((tq, MEM_OUT), lambda b, i: (b * nq + i, 0)),
            pl.BlockSpec((mem_len, 2 * MEM_OUT), lambda b, i: (b, 0)),
        ],
        out_specs=pl.BlockSpec((tq, MEM_OUT), lambda b, i: (b * nq + i, 0)),
        out_shape=jax.ShapeDtypeStruct((t, MEM_OUT), BF16),
        compiler_params=_compiler_params(("parallel", "arbitrary"), vmem),
        name="mem_attn",
    )(q, kv)


def _merge_kernel(x_ref, xb_ref, omla_ref, odiff_ref, omem_ref, wg_ref, bg_ref, wbm_ref, wbd_ref, wbe_ref,
                  wo_ref, g_ref, b_ref, y_ref, yb_ref):
    xb = xb_ref[...]
    merged = None
    for i, (o_ref, wb_ref) in enumerate(((omla_ref, wbm_ref), (odiff_ref, wbd_ref), (omem_ref, wbe_ref))):
        sl = slice(i * D_MODEL, (i + 1) * D_MODEL)
        gate = jax.nn.sigmoid(_dot(xb, wg_ref[:, sl]) + bg_ref[i:i + 1, :])
        term = gate * _dot(o_ref[...], wb_ref[...])
        merged = term if merged is None else merged + term
    z = DEEPNORM_ALPHA * x_ref[...] + _dot(merged.astype(BF16), wo_ref[...])
    y = _layernorm(z, g_ref[...], b_ref[...])
    y_ref[...] = y
    yb_ref[...] = y.astype(BF16)


def _merge(x, xb, omla, odiff, omem, wg, bg, wbm, wbd, wbe, wo, g, b, tm=512):
    t = x.shape[0]
    full = lambda a: pl.BlockSpec(a.shape, lambda i: (0, 0))
    rows = lambda n: pl.BlockSpec((tm, n), lambda i: (i, 0))
    vmem = _vmem_estimate(
        [((tm, D_MODEL), F32), ((tm, D_MODEL), BF16), ((tm, MLA_OUT), BF16), ((tm, DIFF_OUT), BF16),
         ((tm, MEM_OUT), BF16), (wg.shape, BF16), (wbm.shape, BF16), (wbd.shape, BF16), (wbe.shape, BF16),
         (wo.shape, BF16), ((tm, D_MODEL), F32), ((tm, D_MODEL), BF16)],
        [((tm, D_MODEL), F32)] * 5)
    return pl.pallas_call(
        _merge_kernel,
        grid=(t // tm,),
        in_specs=[rows(D_MODEL), rows(D_MODEL), rows(MLA_OUT), rows(DIFF_OUT), rows(MEM_OUT),
                  full(wg), full(bg), full(wbm), full(wbd), full(wbe), full(wo), full(g), full(b)],
        out_specs=[rows(D_MODEL), rows(D_MODEL)],
        out_shape=[jax.ShapeDtypeStruct((t, D_MODEL), F32), jax.ShapeDtypeStruct((t, D_MODEL), BF16)],
        compiler_params=_compiler_params(("parallel",), vmem),
        name="merge",
    )(x, xb, omla, odiff, omem, wg, bg, wbm, wbd, wbe, wo, g, b)


def _mlp_kernel(x_ref, xb_ref, w1_ref, w2_ref, g_ref, b_ref, y_ref, yb_ref, acc_ref):
    kf = pl.program_id(1)

    @pl.when(kf == 0)
    def _():
        acc_ref[...] = jnp.zeros_like(acc_ref)

    h = jnp.maximum(_dot(xb_ref[...], w1_ref[...]), 0.0)
    acc_ref[...] += _dot((h * h).astype(BF16), w2_ref[...])

    @pl.when(kf == pl.num_programs(1) - 1)
    def _():
        y = _layernorm(DEEPNORM_ALPHA * x_ref[...] + acc_ref[...], g_ref[...], b_ref[...])
        y_ref[...] = y
        yb_ref[...] = y.astype(BF16)


def _mlp(x, xb, w1, w2, g, b, tm=1024, tf=1024):
    t = x.shape[0]
    vmem = _vmem_estimate(
        [((tm, D_MODEL), F32), ((tm, D_MODEL), BF16), ((D_MODEL, tf), BF16), ((tf, D_MODEL), BF16),
         ((tm, D_MODEL), F32), ((tm, D_MODEL), BF16)],
        [((tm, D_MODEL), F32), ((tm, tf), F32), ((tm, tf), F32), ((tm, tf), BF16), ((tm, D_MODEL), F32)])
    return pl.pallas_call(
        _mlp_kernel,
        grid=(t // tm, D_FF // tf),
        in_specs=[
            pl.BlockSpec((tm, D_MODEL), lambda i, k: (i, 0)),
            pl.BlockSpec((tm, D_MODEL), lambda i, k: (i, 0)),
            pl.BlockSpec((D_MODEL, tf), lambda i, k: (0, k)),
            pl.BlockSpec((tf, D_MODEL), lambda i, k: (k, 0)),
            pl.BlockSpec((1, D_MODEL), lambda i, k: (0, 0)),
            pl.BlockSpec((1, D_MODEL), lambda i, k: (0, 0)),
        ],
        out_specs=[pl.BlockSpec((tm, D_MODEL), lambda i, k: (i, 0))] * 2,
        out_shape=[jax.ShapeDtypeStruct((t, D_MODEL), F32), jax.ShapeDtypeStruct((t, D_MODEL), BF16)],
        scratch_shapes=[pltpu.VMEM((tm, D_MODEL), F32)],
        compiler_params=_compiler_params(("parallel", "arbitrary"), vmem),
        name="mlp",
    )(x, xb, w1, w2, g, b)


def _lane_patterns():
    lane = jnp.arange(LANE)
    inv_m = ROPE_THETA ** (-jnp.arange(0, MLA_ROPE, 2, dtype=F32) / MLA_ROPE)
    rel = lane - MLA_ROPE_LO
    in_rope = (rel >= 0) & (rel < MLA_ROPE)
    inv_mla = jnp.where(in_rope, inv_m[jnp.clip(rel, 0, MLA_ROPE - 1) % MLA_ROPE_HALF], 0.0)
    sgn_mla = jnp.where(in_rope, jnp.where(rel < MLA_ROPE_HALF, -1.0, 1.0), 0.0)
    inv_d = ROPE_THETA ** (-jnp.arange(0, DIFF_ROT, 2, dtype=F32) / DIFF_ROT)
    rel = lane % DIFF_D
    in_rope = rel < DIFF_ROT
    inv_diff = jnp.where(in_rope, inv_d[rel % DIFF_ROPE_HALF], 0.0)
    sgn_diff = jnp.where(in_rope, jnp.where(rel < DIFF_ROPE_HALF, -1.0, 1.0), 0.0)
    row = lambda a: a.astype(F32).reshape(1, LANE)
    return row(inv_mla), row(sgn_mla), row(inv_diff), row(sgn_diff)


def _layer_weights(w_in, mla_w_uq, mla_w_ukv, w_branch):
    d = w_in.shape[0]
    o = 0
    w_cq = w_in[:, o:o + MLA_Q_LORA]; o += MLA_Q_LORA
    w_ckv = w_in[:, o:o + MLA_KV_LORA]; o += MLA_KV_LORA
    w_kpe = w_in[:, o:o + MLA_ROPE]; o += MLA_ROPE
    w_dq = w_in[:, o:o + DIFF_QK]; o += DIFF_QK
    w_dk = w_in[:, o:o + DIFF_QK]; o += DIFF_QK
    w_dv = w_in[:, o:o + DIFF_OUT]; o += DIFF_OUT
    w_mq = w_in[:, o:o + MEM_OUT]; o += MEM_OUT
    w_gate = w_in[:, o:]
    w_a = jnp.concatenate(
        [w_cq, w_ckv, jnp.zeros((d, MLA_ROPE_LO), F32), w_kpe,
         jnp.zeros((d, LANE - MLA_ROPE_LO - MLA_ROPE), F32)], axis=1)
    qd = MLA_NOPE + MLA_ROPE
    w_uq = jnp.pad(mla_w_uq.reshape(MLA_Q_LORA, MLA_HEADS, qd), ((0, 0), (0, 0), (0, LANE - qd)))
    w_uq = w_uq.reshape(MLA_Q_LORA, MLA_HEADS * LANE)
    ukv = mla_w_ukv.reshape(MLA_KV_LORA, MLA_HEADS, MLA_NOPE + MLA_V)
    w_uk = jnp.pad(ukv[:, :, :MLA_NOPE], ((0, 0), (0, 0), (0, LANE - MLA_NOPE))).reshape(MLA_KV_LORA, MLA_HEADS * LANE)
    w_uv = ukv[:, :, MLA_NOPE:].reshape(MLA_KV_LORA, MLA_OUT)
    wb_mla = w_branch[:MLA_OUT]
    wb_diff = w_branch[MLA_OUT:MLA_OUT + DIFF_OUT]
    wb_mem = w_branch[MLA_OUT + DIFF_OUT:]
    cast = lambda a: a.astype(BF16)
    return tuple(map(cast, (w_a, w_uq, w_uk, w_uv, w_dq, w_dk, w_dv, w_mq, w_gate, wb_mla, wb_diff, wb_mem)))


def kernel(x, mem, positions, w_in, b_gate, mla_q_norm, mla_kv_norm, mla_w_uq, mla_w_ukv, diff_lambda, diff_subln,
           mem_w_kv, w_branch, w_out, ln1_g, ln1_b, mlp_w1, mlp_w2, ln2_g, ln2_b):
    batch, seq, d = x.shape
    mem_len = mem.shape[1]
    t = batch * seq
    depth = w_in.shape[0]

    inv_mla, sgn_mla, inv_diff, sgn_diff = _lane_patterns()
    pos = positions.reshape(t, 1)
    cos_m, sin_m = _rope_tables(pos, inv_mla, sgn_mla)
    cos_d, sin_d = _rope_tables(pos, inv_diff, sgn_diff)

    xf = x.reshape(t, d)
    xb = xf.astype(BF16)
    memb = mem.reshape(batch * mem_len, d).astype(BF16)
    row = lambda a: a.reshape(1, -1)

    for l in range(depth):
        lambda_init = 0.8 - 0.6 * math.exp(-0.3 * l)
        (w_a, w_uq, w_uk, w_uv, w_dq, w_dk, w_dv, w_mq, w_gate, wb_mla, wb_diff, wb_mem) = _layer_weights(
            w_in[l], mla_w_uq[l], mla_w_ukv[l], w_branch[l])

        q_m, k_m, v_m = _mla_proj(xb, w_a, row(mla_q_norm[l]), row(mla_kv_norm[l]), w_uq, w_uk, w_uv, cos_m, sin_m)
        o_mla = _mla_attn(q_m, k_m, v_m, batch, seq)

        q_d = _proj(xb, w_dq, cos_d, sin_d, scale=LOG2E * DIFF_D ** -0.5, name="diff_q_proj")
        k_d = _proj(xb, w_dk, cos_d, sin_d, name="diff_k_proj")
        v_d = _proj(xb, w_dv, ones_ext=True, name="diff_v_proj")
        o_diff = _diff_attn(q_d, k_d, v_d, diff_lambda[l], row(diff_subln[l]), lambda_init, batch, seq)

        q_e = _proj(xb, w_mq, scale=MEM_HD ** -0.5, name="mem_q_proj")
        kv_e = _proj(memb, mem_w_kv[l].astype(BF16), tm=mem_len, name="mem_kv_proj")
        o_mem = _mem_attn(q_e, kv_e, batch, seq, mem_len)

        xf, xb = _merge(xf, xb, o_mla, o_diff, o_mem, w_gate, b_gate[l], wb_mla, wb_diff, wb_mem,
                        w_out[l].astype(BF16), row(ln1_g[l]), row(ln1_b[l]))
        xf, xb = _mlp(xf, xb, mlp_w1[l].astype(BF16), mlp_w2[l].astype(BF16), row(ln2_g[l]), row(ln2_b[l]))
    return xf.reshape(batch, seq, d)
```

```python
import functools
import math

import jax
import jax.numpy as jnp
from jax import lax
from jax.experimental import pallas as pl
from jax.experimental.pallas import tpu as pltpu

D_MODEL = 1024
DEPTH = 4
ROPE_THETA = 500000.0
MLA_HEADS = 8
MLA_NOPE = 64
MLA_ROPE = 32
MLA_V = 64
MLA_Q_LORA = 384
MLA_KV_LORA = 256
DIFF_HEADS = 8
DIFF_D = 64
DIFF_ROT = DIFF_D // 4
MEM_HEADS = 4
MEM_HD = 128
N_BRANCH = 3
D_FF = 4 * D_MODEL
DEEPNORM_ALPHA = (2 * DEPTH) ** 0.25
LN_EPS = 1e-5
MLA_OUT = MLA_HEADS * MLA_V
DIFF_QK = DIFF_HEADS * 2 * DIFF_D
DIFF_OUT = DIFF_HEADS * 2 * DIFF_D
MEM_OUT = MEM_HEADS * MEM_HD

LANE = 128
V7X_VMEM_BYTES = 64 * 1024 * 1024

MLA_ROPE_LO = MLA_NOPE
MLA_ROPE_HALF = MLA_ROPE // 2
DIFF_ROPE_HALF = DIFF_ROT // 2

BF16 = jnp.bfloat16
F32 = jnp.float32
LOG2E = math.log2(math.e)


def _dot(a, b):
    return jnp.dot(a, b, preferred_element_type=F32)


def _dot_nt(a, b):
    return lax.dot_general(a, b, (((1,), (1,)), ((), ())), preferred_element_type=F32)


def _compiler_params(semantics, vmem_bytes):
    limit = min(int(vmem_bytes), V7X_VMEM_BYTES - 8 * 1024 * 1024)
    return pltpu.CompilerParams(dimension_semantics=semantics, vmem_limit_bytes=limit)


def _nbytes(shape, dtype):
    return math.prod(shape) * jnp.dtype(dtype).itemsize


def _vmem_estimate(blocks, temps):
    return 2 * sum(_nbytes(s, d) for s, d in blocks) + sum(_nbytes(s, d) for s, d in temps) + (4 << 20)


def _rope_rotate(y, cos, sin_signed, x1_mask, half):
    partner = jnp.where(x1_mask, pltpu.roll(y, LANE - half, 1), pltpu.roll(y, half, 1))
    return y * cos + partner * sin_signed


def _x1_mask_mla(rows):
    lane = lax.broadcasted_iota(jnp.int32, (rows, LANE), 1)
    return (lane >= MLA_ROPE_LO) & (lane < MLA_ROPE_LO + MLA_ROPE_HALF)


def _x1_mask_diff(rows):
    lane = lax.broadcasted_iota(jnp.int32, (rows, LANE), 1) % DIFF_D
    return lane < DIFF_ROPE_HALF


def _layernorm(z, g, b):
    mu = jnp.mean(z, axis=-1, keepdims=True)
    zc = z - mu
    var = jnp.mean(zc * zc, axis=-1, keepdims=True)
    return zc * lax.rsqrt(var + LN_EPS) * g + b


def _rope_table_kernel(pos_ref, inv_ref, sgn_ref, cos_ref, sin_ref):
    ang = pos_ref[...].astype(F32) * inv_ref[...]
    cos_ref[...] = jnp.cos(ang)
    sin_ref[...] = jnp.sin(ang) * sgn_ref[...]


def _rope_tables(pos, inv_lane, sgn_lane, tm=2048):
    t = pos.shape[0]
    return pl.pallas_call(
        _rope_table_kernel,
        grid=(t // tm,),
        in_specs=[
            pl.BlockSpec((tm, 1), lambda i: (i, 0)),
            pl.BlockSpec((1, LANE), lambda i: (0, 0)),
            pl.BlockSpec((1, LANE), lambda i: (0, 0)),
        ],
        out_specs=[pl.BlockSpec((tm, LANE), lambda i: (i, 0))] * 2,
        out_shape=[jax.ShapeDtypeStruct((t, LANE), F32)] * 2,
        compiler_params=_compiler_params(("parallel",), 32 << 20),
        name="rope_tables",
    )(pos, inv_lane, sgn_lane)


def _mla_proj_kernel(x_ref, wa_ref, gq_ref, gkv_ref, wuq_ref, wuk_ref, wuv_ref, cos_ref, sin_ref,
                     q_ref, k_ref, v_ref, *, scale):
    tm = x_ref.shape[0]
    h = _dot(x_ref[...], wa_ref[...])
    cq = h[:, :MLA_Q_LORA]
    ckv = h[:, MLA_Q_LORA:MLA_Q_LORA + MLA_KV_LORA]
    kpe = h[:, MLA_Q_LORA + MLA_KV_LORA:]

    def rms(v, g):
        return (v * lax.rsqrt(jnp.mean(v * v, axis=-1, keepdims=True) + 1e-6) * g).astype(BF16)

    cqn = rms(cq, gq_ref[...])
    ckvn = rms(ckv, gkv_ref[...])
    cos = cos_ref[...]
    sin = sin_ref[...]
    x1 = _x1_mask_mla(tm)
    kpe_rot = _rope_rotate(kpe, cos, sin, x1, MLA_ROPE_HALF)

    q = _dot(cqn, wuq_ref[...])
    kn = _dot(ckvn, wuk_ref[...])
    for hd in range(MLA_HEADS):
        sl = slice(hd * LANE, (hd + 1) * LANE)
        qh = _rope_rotate(q[:, sl], cos, sin, x1, MLA_ROPE_HALF) * scale
        q_ref[:, sl] = qh.astype(BF16)
        k_ref[:, sl] = (kn[:, sl] + kpe_rot).astype(BF16)
    v = _dot(ckvn, wuv_ref[...]).astype(BF16)
    ones = jnp.ones((tm, LANE), BF16)
    for pair in range(MLA_HEADS // 2):
        v_ref[:, 2 * pair * LANE:(2 * pair + 1) * LANE] = v[:, pair * LANE:(pair + 1) * LANE]
        v_ref[:, (2 * pair + 1) * LANE:(2 * pair + 2) * LANE] = ones


def _mla_proj(xb, wa, gq, gkv, wuq, wuk, wuv, cos, sin, tm=512):
    t = xb.shape[0]
    na = wa.shape[1]
    nq = MLA_HEADS * LANE
    nv = MLA_HEADS * LANE
    full = lambda shape: pl.BlockSpec(shape, lambda i: (0, 0))
    rows = lambda n: pl.BlockSpec((tm, n), lambda i: (i, 0))
    vmem = _vmem_estimate(
        [((tm, D_MODEL), BF16), (wa.shape, BF16), (wuq.shape, BF16), (wuk.shape, BF16), (wuv.shape, BF16),
         ((tm, LANE), F32), ((tm, LANE), F32), ((tm, nq), BF16), ((tm, nq), BF16), ((tm, nv), BF16)],
        [((tm, na), F32), ((tm, nq), F32), ((tm, nq), F32), ((tm, nq), F32)])
    return pl.pallas_call(
        functools.partial(_mla_proj_kernel, scale=LOG2E * (MLA_NOPE + MLA_ROPE) ** -0.5),
        grid=(t // tm,),
        in_specs=[rows(D_MODEL), full(wa.shape), full(gq.shape), full(gkv.shape), full(wuq.shape),
                  full(wuk.shape), full(wuv.shape), rows(LANE), rows(LANE)],
        out_specs=[rows(nq), rows(nq), rows(nv)],
        out_shape=[jax.ShapeDtypeStruct((t, nq), BF16), jax.ShapeDtypeStruct((t, nq), BF16),
                   jax.ShapeDtypeStruct((t, nv), BF16)],
        compiler_params=_compiler_params(("parallel",), vmem),
        name="mla_proj",
    )(xb, wa, gq, gkv, wuq, wuk, wuv, cos, sin)


def _branch_proj_kernel(x_ref, w_ref, cos_ref, sin_ref, qd_ref, kd_ref, vd_ref, qe_ref, *, q_scale, mem_scale):
    tm = x_ref.shape[0]
    x = x_ref[...]
    cos = cos_ref[...]
    sin = sin_ref[...]
    x1 = _x1_mask_diff(tm)

    def rotated(y, scale, o_ref):
        for hd in range(DIFF_HEADS):
            sl = slice(hd * LANE, (hd + 1) * LANE)
            yh = _rope_rotate(y[:, sl], cos, sin, x1, DIFF_ROPE_HALF)
            o_ref[:, sl] = (yh if scale == 1.0 else yh * scale).astype(o_ref.dtype)

    rotated(_dot(x, w_ref[:, :DIFF_QK]), q_scale, qd_ref)
    rotated(_dot(x, w_ref[:, DIFF_QK:2 * DIFF_QK]), 1.0, kd_ref)
    v = _dot(x, w_ref[:, 2 * DIFF_QK:2 * DIFF_QK + DIFF_OUT]).astype(vd_ref.dtype)
    ones = jnp.ones((tm, LANE), vd_ref.dtype)
    for hd in range(DIFF_HEADS):
        vd_ref[:, 2 * hd * LANE:(2 * hd + 1) * LANE] = v[:, hd * LANE:(hd + 1) * LANE]
        vd_ref[:, (2 * hd + 1) * LANE:(2 * hd + 2) * LANE] = ones
    qe_ref[...] = (_dot(x, w_ref[:, 2 * DIFF_QK + DIFF_OUT:]) * mem_scale).astype(qe_ref.dtype)


def _branch_proj(xb, w, cos, sin, tm=512):
    t, kdim = xb.shape
    n = w.shape[1]
    rows = lambda width: pl.BlockSpec((tm, width), lambda i: (i, 0))
    widths = (DIFF_QK, DIFF_QK, 2 * DIFF_OUT, MEM_OUT)
    vmem = _vmem_estimate(
        [((tm, kdim), BF16), ((kdim, n), BF16), ((tm, LANE), F32), ((tm, LANE), F32)]
        + [((tm, width), BF16) for width in widths],
        [((tm, DIFF_QK), F32)] * 6)
    return pl.pallas_call(
        functools.partial(_branch_proj_kernel, q_scale=LOG2E * DIFF_D ** -0.5, mem_scale=MEM_HD ** -0.5),
        grid=(t // tm,),
        in_specs=[rows(kdim), pl.BlockSpec((kdim, n), lambda i: (0, 0)), rows(LANE), rows(LANE)],
        out_specs=[rows(width) for width in widths],
        out_shape=[jax.ShapeDtypeStruct((t, width), BF16) for width in widths],
        compiler_params=_compiler_params(("parallel",), vmem),
        name="branch_proj",
    )(xb, w, cos, sin)


def _mem_kv_kernel(x_ref, w_ref, o_ref):
    o_ref[...] = _dot(x_ref[...], w_ref[...]).astype(o_ref.dtype)


def _mem_kv_proj(memb, w_stack, layer, tm):
    t, kdim = memb.shape
    n = w_stack.shape[2]
    vmem = _vmem_estimate([((tm, kdim), BF16), ((kdim, n), BF16), ((tm, n), BF16)], [((tm, n), F32)] * 2)
    return pl.pallas_call(
        _mem_kv_kernel,
        grid=(t // tm,),
        in_specs=[pl.BlockSpec((tm, kdim), lambda i: (i, 0)),
                  pl.BlockSpec((None, kdim, n), lambda i: (layer, 0, 0))],
        out_specs=pl.BlockSpec((tm, n), lambda i: (i, 0)),
        out_shape=jax.ShapeDtypeStruct((t, n), BF16),
        compiler_params=_compiler_params(("parallel",), vmem),
        name="mem_kv_proj",
    )(memb, w_stack)


def _softmax_pv(q, k, v_ext):
    s = _dot_nt(q, k)
    m = jnp.max(s, axis=-1, keepdims=True)
    p = jnp.exp2((s - m).astype(BF16))
    pv = _dot(p, v_ext)
    return pv[:, :LANE] / pv[:, LANE:]


def _mla_attn_kernel(q_ref, k_ref, v_ref, o_ref, *, pairs):
    tq = q_ref.shape[0]
    first = lax.broadcasted_iota(jnp.int32, (tq, LANE), 1) < MLA_V
    for pr in range(pairs):
        v_ext = v_ref[:, 2 * pr * LANE:(2 * pr + 2) * LANE]
        outs = []
        for j in range(2):
            sl = slice((2 * pr + j) * LANE, (2 * pr + j + 1) * LANE)
            outs.append(_softmax_pv(q_ref[:, sl], k_ref[:, sl], v_ext))
        o_ref[:, pr * LANE:(pr + 1) * LANE] = jnp.where(first, outs[0], outs[1]).astype(o_ref.dtype)


def _mla_attn(q, k, v, batch, seq, tq=512, pairs=4):
    t = q.shape[0]
    nq = seq // tq
    wide = 2 * pairs * LANE
    grid = (batch, MLA_HEADS // (2 * pairs), nq)
    vmem = _vmem_estimate(
        [((tq, wide), BF16), ((seq, wide), BF16), ((seq, wide), BF16), ((tq, pairs * LANE), BF16)],
        ([((tq, seq), F32)] * 2 + [((tq, seq), BF16)] * 2) * 2 * pairs)
    return pl.pallas_call(
        functools.partial(_mla_attn_kernel, pairs=pairs),
        grid=grid,
        in_specs=[
            pl.BlockSpec((tq, wide), lambda b, g, i: (b * nq + i, g)),
            pl.BlockSpec((seq, wide), lambda b, g, i: (b, g)),
            pl.BlockSpec((seq, wide), lambda b, g, i: (b, g)),
        ],
        out_specs=pl.BlockSpec((tq, pairs * LANE), lambda b, g, i: (b * nq + i, g)),
        out_shape=jax.ShapeDtypeStruct((t, MLA_OUT), BF16),
        compiler_params=_compiler_params(("parallel", "parallel", "arbitrary"), vmem),
        name="mla_attn",
    )(q, k, v)


def _diff_attn_kernel(q_ref, k_ref, v_ref, lam_ref, g_ref, o_ref, *, lambda_init, heads):
    tq = q_ref.shape[0]
    lam = lam_ref[...]
    lam_a = jnp.sum(lam[0:1] * lam[1:2], axis=-1, keepdims=True)
    lam_b = jnp.sum(lam[2:3] * lam[3:4], axis=-1, keepdims=True)
    lambda_full = jnp.exp(lam_a) - jnp.exp(lam_b) + lambda_init
    map0 = lax.broadcasted_iota(jnp.int32, (tq, LANE), 1) < DIFF_D
    gain = g_ref[...] * (1.0 - lambda_init)
    for hd in range(heads):
        sl = slice(hd * LANE, (hd + 1) * LANE)
        q = q_ref[:, sl]
        k = k_ref[:, sl]
        v_ext = v_ref[:, 2 * hd * LANE:(2 * hd + 2) * LANE]
        zero = jnp.zeros_like(q)
        o = _softmax_pv(jnp.where(map0, q, zero), k, v_ext) - lambda_full * _softmax_pv(
            jnp.where(map0, zero, q), k, v_ext)
        o = o * lax.rsqrt(jnp.mean(o * o, axis=-1, keepdims=True) + 1e-5) * gain
        o_ref[:, sl] = o.astype(o_ref.dtype)


def _diff_attn(q, k, v, lam, subln, lambda_init, batch, seq, tq=256, heads=8):
    t = q.shape[0]
    nq = seq // tq
    wide = heads * LANE
    vmem = _vmem_estimate(
        [((tq, wide), BF16), ((seq, wide), BF16), ((seq, 2 * wide), BF16), ((tq, wide), BF16)],
        ([((tq, seq), F32)] * 2 + [((tq, seq), BF16)] * 2) * 2 * heads)
    return pl.pallas_call(
        functools.partial(_diff_attn_kernel, lambda_init=lambda_init, heads=heads),
        grid=(batch, DIFF_HEADS // heads, nq),
        in_specs=[
            pl.BlockSpec((tq, wide), lambda b, h, i: (b * nq + i, h)),
            pl.BlockSpec((seq, wide), lambda b, h, i: (b, h)),
            pl.BlockSpec((seq, 2 * wide), lambda b, h, i: (b, h)),
            pl.BlockSpec(lam.shape, lambda b, h, i: (0, 0)),
            pl.BlockSpec(subln.shape, lambda b, h, i: (0, 0)),
        ],
        out_specs=pl.BlockSpec((tq, wide), lambda b, h, i: (b * nq + i, h)),
        out_shape=jax.ShapeDtypeStruct((t, DIFF_OUT), BF16),
        compiler_params=_compiler_params(("parallel", "parallel", "arbitrary"), vmem),
        name="diff_attn",
    )(q, k, v, lam, subln)


def _mem_attn_kernel(q_ref, kv_ref, o_ref):
    for hd in range(MEM_HEADS):
        sl = slice(hd * MEM_HD, (hd + 1) * MEM_HD)
        vsl = slice(MEM_OUT + hd * MEM_HD, MEM_OUT + (hd + 1) * MEM_HD)
        s = _dot_nt(q_ref[:, sl], kv_ref[:, sl])
        p = jnp.exp(s - jnp.max(s, axis=-1, keepdims=True))
        inv_l = 1.0 / jnp.sum(p, axis=-1, keepdims=True)
        o_ref[:, sl] = (_dot(p.astype(BF16), kv_ref[:, vsl]) * inv_l).astype(o_ref.dtype)


def _mem_attn(q, kv, batch, seq, mem_len, tq=1024):
    t = q.shape[0]
    nq = seq // tq
    vmem = _vmem_estimate(
        [((tq, MEM_OUT), BF16), ((mem_len, 2 * MEM_OUT), BF16), ((tq, MEM_OUT), BF16)],
        [((tq, mem_len), F32)] * 3 + [((tq, MEM_OUT), F32)])
    return pl.pallas_call(
        _mem_attn_kernel,
        grid=(batch, nq),
        in_specs=[
            pl.BlockSpec((tq, MEM_OUT), lambda b, i: (b * nq + i, 0)),
            pl.BlockSpec((mem_len, 2 * MEM_OUT), lambda b, i: (b, 0)),
        ],
        out_specs=pl.BlockSpec((tq, MEM_OUT), lambda b, i: (b * nq + i, 0)),
        out_shape=jax.ShapeDtypeStruct((t, MEM_OUT), BF16),
        compiler_params=_compiler_params(("parallel", "arbitrary"), vmem),
        name="mem_attn",
    )(q, kv)


MERGE_CHUNKS = 2


def _merge_kernel(x_ref, xb_ref, omla_ref, odiff_ref, omem_ref, wg_ref, bg_ref, wb_ref, wo_ref, g_ref, b_ref,
                  y_ref, yb_ref):
    chunk = x_ref.shape[0] // MERGE_CHUNKS
    for c in range(MERGE_CHUNKS):
        rows = slice(c * chunk, (c + 1) * chunk)
        xb = xb_ref[rows, :]
        merged = None
        row = 0
        for i, o_ref in enumerate((omla_ref, odiff_ref, omem_ref)):
            sl = slice(i * D_MODEL, (i + 1) * D_MODEL)
            width = o_ref.shape[1]
            gate = jax.nn.sigmoid(_dot(xb, wg_ref[:, sl]) + bg_ref[i:i + 1, :])
            term = gate * _dot(o_ref[rows, :], wb_ref[row:row + width, :])
            merged = term if merged is None else merged + term
            row += width
        z = DEEPNORM_ALPHA * x_ref[rows, :] + _dot(merged.astype(BF16), wo_ref[...])
        y = _layernorm(z, g_ref[...], b_ref[...])
        y_ref[rows, :] = y
        yb_ref[rows, :] = y.astype(BF16)


def _layer_block(stack, layer):
    return pl.BlockSpec((None,) + stack.shape[1:], lambda *_: (layer, 0, 0))


def _merge(x, xb, omla, odiff, omem, wg, bg_stack, wb_stack, wo_stack, g_stack, b_stack, layer, tm=512):
    t = x.shape[0]
    rows = lambda n: pl.BlockSpec((tm, n), lambda i: (i, 0))
    vmem = _vmem_estimate(
        [((tm, D_MODEL), F32), ((tm, D_MODEL), BF16), ((tm, MLA_OUT), BF16), ((tm, DIFF_OUT), BF16),
         ((tm, MEM_OUT), BF16), (wg.shape, BF16), (wb_stack.shape[1:], BF16), (wo_stack.shape[1:], BF16),
         ((tm, D_MODEL), F32), ((tm, D_MODEL), BF16)],
        [((tm, D_MODEL), F32)] * 5)
    return pl.pallas_call(
        _merge_kernel,
        grid=(t // tm,),
        in_specs=[rows(D_MODEL), rows(D_MODEL), rows(MLA_OUT), rows(DIFF_OUT), rows(MEM_OUT),
                  pl.BlockSpec(wg.shape, lambda i: (0, 0)), _layer_block(bg_stack, layer),
                  _layer_block(wb_stack, layer), _layer_block(wo_stack, layer),
                  _layer_block(g_stack, layer), _layer_block(b_stack, layer)],
        out_specs=[rows(D_MODEL), rows(D_MODEL)],
        out_shape=[jax.ShapeDtypeStruct((t, D_MODEL), F32), jax.ShapeDtypeStruct((t, D_MODEL), BF16)],
        compiler_params=_compiler_params(("parallel",), vmem),
        name="merge",
    )(x, xb, omla, odiff, omem, wg, bg_stack, wb_stack, wo_stack, g_stack, b_stack)


MLP_EPILOGUE_CHUNKS = 4


def _mlp_kernel(x_ref, xb_ref, w1_ref, w2_ref, g_ref, b_ref, y_ref, yb_ref, acc_ref):
    kf = pl.program_id(1)

    def hidden():
        h = jnp.maximum(_dot(xb_ref[...], w1_ref[...]), 0.0)
        return (h * h).astype(BF16)

    last = pl.num_programs(1) - 1

    @pl.when(kf == 0)
    def _():
        acc_ref[...] = _dot(hidden(), w2_ref[...])

    @pl.when((kf > 0) & (kf < last))
    def _():
        acc_ref[...] += _dot(hidden(), w2_ref[...])

    @pl.when(kf == last)
    def _():
        hh = hidden()
        chunk = x_ref.shape[0] // MLP_EPILOGUE_CHUNKS
        for c in range(MLP_EPILOGUE_CHUNKS):
            rows = slice(c * chunk, (c + 1) * chunk)
            z = DEEPNORM_ALPHA * x_ref[rows, :] + (acc_ref[rows, :] + _dot(hh[rows, :], w2_ref[...]))
            y = _layernorm(z, g_ref[...], b_ref[...])
            y_ref[rows, :] = y
            yb_ref[rows, :] = y.astype(BF16)


def _mlp(x, xb, w1_stack, w2_stack, g_stack, b_stack, layer, tm=1024, tf=1024):
    t = x.shape[0]
    vmem = _vmem_estimate(
        [((tm, D_MODEL), F32), ((tm, D_MODEL), BF16), ((D_MODEL, tf), BF16), ((tf, D_MODEL), BF16),
         ((tm, D_MODEL), F32), ((tm, D_MODEL), BF16)],
        [((tm, D_MODEL), F32), ((tm, tf), F32), ((tm, tf), F32), ((tm, tf), BF16), ((tm, D_MODEL), F32)])
    return pl.pallas_call(
        _mlp_kernel,
        grid=(t // tm, D_FF // tf),
        in_specs=[
            pl.BlockSpec((tm, D_MODEL), lambda i, k: (i, 0)),
            pl.BlockSpec((tm, D_MODEL), lambda i, k: (i, 0)),
            pl.BlockSpec((None, D_MODEL, tf), lambda i, k: (layer, 0, k)),
            pl.BlockSpec((None, tf, D_MODEL), lambda i, k: (layer, k, 0)),
            _layer_block(g_stack, layer),
            _layer_block(b_stack, layer),
        ],
        out_specs=[pl.BlockSpec((tm, D_MODEL), lambda i, k: (i, 0))] * 2,
        out_shape=[jax.ShapeDtypeStruct((t, D_MODEL), F32), jax.ShapeDtypeStruct((t, D_MODEL), BF16)],
        scratch_shapes=[pltpu.VMEM((tm, D_MODEL), F32)],
        compiler_params=_compiler_params(("parallel", "arbitrary"), vmem),
        name="mlp",
    )(x, xb, w1_stack, w2_stack, g_stack, b_stack)


def _lane_patterns():
    lane = jnp.arange(LANE)
    inv_m = ROPE_THETA ** (-jnp.arange(0, MLA_ROPE, 2, dtype=F32) / MLA_ROPE)
    rel = lane - MLA_ROPE_LO
    in_rope = (rel >= 0) & (rel < MLA_ROPE)
    inv_mla = jnp.where(in_rope, inv_m[jnp.clip(rel, 0, MLA_ROPE - 1) % MLA_ROPE_HALF], 0.0)
    sgn_mla = jnp.where(in_rope, jnp.where(rel < MLA_ROPE_HALF, -1.0, 1.0), 0.0)
    inv_d = ROPE_THETA ** (-jnp.arange(0, DIFF_ROT, 2, dtype=F32) / DIFF_ROT)
    rel = lane % DIFF_D
    in_rope = rel < DIFF_ROT
    inv_diff = jnp.where(in_rope, inv_d[rel % DIFF_ROPE_HALF], 0.0)
    sgn_diff = jnp.where(in_rope, jnp.where(rel < DIFF_ROPE_HALF, -1.0, 1.0), 0.0)
    row = lambda a: a.astype(F32).reshape(1, LANE)
    return row(inv_mla), row(sgn_mla), row(inv_diff), row(sgn_diff)


def _in_proj_weights(w_in_b, mla_w_uq, mla_w_ukv):
    d = w_in_b.shape[0]
    lat = MLA_Q_LORA + MLA_KV_LORA
    branch_lo = lat + MLA_ROPE
    gate_lo = branch_lo + 2 * DIFF_QK + DIFF_OUT + MEM_OUT
    w_a = jnp.concatenate(
        [w_in_b[:, :lat], jnp.zeros((d, MLA_ROPE_LO), BF16), w_in_b[:, lat:branch_lo],
         jnp.zeros((d, LANE - MLA_ROPE_LO - MLA_ROPE), BF16)], axis=1)
    w_branch_in = w_in_b[:, branch_lo:gate_lo]
    w_gate = w_in_b[:, gate_lo:]
    qd = MLA_NOPE + MLA_ROPE
    w_uq = jnp.pad(mla_w_uq.reshape(MLA_Q_LORA, MLA_HEADS, qd), ((0, 0), (0, 0), (0, LANE - qd)))
    w_uq = w_uq.reshape(MLA_Q_LORA, MLA_HEADS * LANE).astype(BF16)
    ukv = mla_w_ukv.reshape(MLA_KV_LORA, MLA_HEADS, MLA_NOPE + MLA_V)
    w_uk = jnp.pad(ukv[:, :, :MLA_NOPE], ((0, 0), (0, 0), (0, LANE - MLA_NOPE)))
    w_uk = w_uk.reshape(MLA_KV_LORA, MLA_HEADS * LANE).astype(BF16)
    w_uv = ukv[:, :, MLA_NOPE:].reshape(MLA_KV_LORA, MLA_OUT).astype(BF16)
    return w_a, w_branch_in, w_gate, w_uq, w_uk, w_uv


def kernel(x, mem, positions, w_in, b_gate, mla_q_norm, mla_kv_norm, mla_w_uq, mla_w_ukv, diff_lambda, diff_subln,
           mem_w_kv, w_branch, w_out, ln1_g, ln1_b, mlp_w1, mlp_w2, ln2_g, ln2_b):
    batch, seq, d = x.shape
    mem_len = mem.shape[1]
    t = batch * seq
    depth = w_in.shape[0]

    inv_mla, sgn_mla, inv_diff, sgn_diff = _lane_patterns()
    pos = positions.reshape(t, 1)
    cos_m, sin_m = _rope_tables(pos, inv_mla, sgn_mla)
    cos_d, sin_d = _rope_tables(pos, inv_diff, sgn_diff)

    xf = x.reshape(t, d)
    xb = xf.astype(BF16)
    memb = mem.reshape(batch * mem_len, d).astype(BF16)
    row = lambda a: a.reshape(1, -1)
    stack_rows = lambda a: a.reshape(depth, 1, -1)

    w_in_b, mem_w_kv_b, w_branch_b, w_out_b, w1_b, w2_b = (
        a.astype(BF16) for a in (w_in, mem_w_kv, w_branch, w_out, mlp_w1, mlp_w2))
    ln1_g_s, ln1_b_s, ln2_g_s, ln2_b_s = map(stack_rows, (ln1_g, ln1_b, ln2_g, ln2_b))

    for l in range(depth):
        lambda_init = 0.8 - 0.6 * math.exp(-0.3 * l)
        w_a, w_branch_in, w_gate, w_uq, w_uk, w_uv = _in_proj_weights(w_in_b[l], mla_w_uq[l], mla_w_ukv[l])

        q_m, k_m, v_m = _mla_proj(xb, w_a, row(mla_q_norm[l]), row(mla_kv_norm[l]), w_uq, w_uk, w_uv, cos_m, sin_m)
        o_mla = _mla_attn(q_m, k_m, v_m, batch, seq)

        q_d, k_d, v_d, q_e = _branch_proj(xb, w_branch_in, cos_d, sin_d)
        o_diff = _diff_attn(q_d, k_d, v_d, diff_lambda[l], row(diff_subln[l]), lambda_init, batch, seq)

        kv_e = _mem_kv_proj(memb, mem_w_kv_b, l, tm=mem_len)
        o_mem = _mem_attn(q_e, kv_e, batch, seq, mem_len)

        xf, xb = _merge(xf, xb, o_mla, o_diff, o_mem, w_gate, b_gate, w_branch_b, w_out_b, ln1_g_s, ln1_b_s, l)
        xf, xb = _mlp(xf, xb, w1_b, w2_b, ln2_g_s, ln2_b_s, l)
    return xf.reshape(batch, seq, d)
```

```python
import functools
import math

import jax
import jax.numpy as jnp
from jax import lax
from jax.experimental import pallas as pl
from jax.experimental.pallas import tpu as pltpu

D_MODEL = 1024
DEPTH = 4
ROPE_THETA = 500000.0
MLA_HEADS = 8
MLA_NOPE = 64
MLA_ROPE = 32
MLA_V = 64
MLA_Q_LORA = 384
MLA_KV_LORA = 256
DIFF_HEADS = 8
DIFF_D = 64
DIFF_ROT = DIFF_D // 4
MEM_HEADS = 4
MEM_HD = 128
N_BRANCH = 3
D_FF = 4 * D_MODEL
DEEPNORM_ALPHA = (2 * DEPTH) ** 0.25
LN_EPS = 1e-5
MLA_OUT = MLA_HEADS * MLA_V
DIFF_QK = DIFF_HEADS * 2 * DIFF_D
DIFF_OUT = DIFF_HEADS * 2 * DIFF_D
MEM_OUT = MEM_HEADS * MEM_HD

LANE = 128
V7X_VMEM_BYTES = 64 * 1024 * 1024

MLA_ROPE_LO = MLA_NOPE
MLA_ROPE_HALF = MLA_ROPE // 2
DIFF_ROPE_HALF = DIFF_ROT // 2

BF16 = jnp.bfloat16
F32 = jnp.float32
LOG2E = math.log2(math.e)


def _dot(a, b):
    return jnp.dot(a, b, preferred_element_type=F32)


def _dot_nt(a, b):
    return lax.dot_general(a, b, (((1,), (1,)), ((), ())), preferred_element_type=F32)


def _compiler_params(semantics, vmem_bytes):
    limit = min(int(vmem_bytes), V7X_VMEM_BYTES - 8 * 1024 * 1024)
    return pltpu.CompilerParams(dimension_semantics=semantics, vmem_limit_bytes=limit)


def _nbytes(shape, dtype):
    return math.prod(shape) * jnp.dtype(dtype).itemsize


def _vmem_estimate(blocks, temps):
    return 2 * sum(_nbytes(s, d) for s, d in blocks) + sum(_nbytes(s, d) for s, d in temps) + (4 << 20)


def _rope_rotate(y, cos, sin_signed, x1_mask, half):
    partner = jnp.where(x1_mask, pltpu.roll(y, LANE - half, 1), pltpu.roll(y, half, 1))
    return y * cos + partner * sin_signed


def _x1_mask_mla(rows):
    lane = lax.broadcasted_iota(jnp.int32, (rows, LANE), 1)
    return (lane >= MLA_ROPE_LO) & (lane < MLA_ROPE_LO + MLA_ROPE_HALF)


def _x1_mask_diff(rows):
    lane = lax.broadcasted_iota(jnp.int32, (rows, LANE), 1) % DIFF_D
    return lane < DIFF_ROPE_HALF


def _layernorm(z, g, b):
    mu = jnp.mean(z, axis=-1, keepdims=True)
    zc = z - mu
    var = jnp.mean(zc * zc, axis=-1, keepdims=True)
    return zc * lax.rsqrt(var + LN_EPS) * g + b


def _rope_table_kernel(pos_ref, inv_ref, sgn_ref, cos_ref, sin_ref):
    ang = pos_ref[...].astype(F32) * inv_ref[...]
    cos_ref[...] = jnp.cos(ang)
    sin_ref[...] = jnp.sin(ang) * sgn_ref[...]


def _rope_tables(pos, inv_lane, sgn_lane, tm=2048):
    t = pos.shape[0]
    return pl.pallas_call(
        _rope_table_kernel,
        grid=(t // tm,),
        in_specs=[
            pl.BlockSpec((tm, 1), lambda i: (i, 0)),
            pl.BlockSpec((1, LANE), lambda i: (0, 0)),
            pl.BlockSpec((1, LANE), lambda i: (0, 0)),
        ],
        out_specs=[pl.BlockSpec((tm, LANE), lambda i: (i, 0))] * 2,
        out_shape=[jax.ShapeDtypeStruct((t, LANE), F32)] * 2,
        compiler_params=_compiler_params(("parallel",), 32 << 20),
        name="rope_tables",
    )(pos, inv_lane, sgn_lane)


def _mla_proj_kernel(x_ref, wa_ref, gq_ref, gkv_ref, wuq_ref, wuqp_ref, wuk_ref, wuv_ref, cos_ref, sin_ref,
                     q_ref, k_ref, v_ref, *, scale):
    tm = x_ref.shape[0]
    h = _dot(x_ref[...], wa_ref[...])
    cq = h[:, :MLA_Q_LORA]
    ckv = h[:, MLA_Q_LORA:MLA_Q_LORA + MLA_KV_LORA]
    kpe = h[:, MLA_Q_LORA + MLA_KV_LORA:]

    def rms(v, g):
        return (v * lax.rsqrt(jnp.mean(v * v, axis=-1, keepdims=True) + 1e-6) * g).astype(BF16)

    cqn = rms(cq, gq_ref[...])
    ckvn = rms(ckv, gkv_ref[...])
    cos = cos_ref[...]
    sin = sin_ref[...]
    kpe_rot = _rope_rotate(kpe, cos, sin, _x1_mask_mla(tm), MLA_ROPE_HALF)

    q = _dot(cqn, wuq_ref[...])
    q_partner = _dot(cqn, wuqp_ref[...])
    kn = _dot(ckvn, wuk_ref[...])
    cos_q = cos * scale
    sin_q = sin * scale
    for hd in range(MLA_HEADS):
        sl = slice(hd * LANE, (hd + 1) * LANE)
        q_ref[:, sl] = (q[:, sl] * cos_q + q_partner[:, sl] * sin_q).astype(BF16)
        k_ref[:, sl] = (kn[:, sl] + kpe_rot).astype(BF16)
    v = _dot(ckvn, wuv_ref[...]).astype(BF16)
    ones = jnp.ones((tm, LANE), BF16)
    for pair in range(MLA_HEADS // 2):
        v_ref[:, 2 * pair * LANE:(2 * pair + 1) * LANE] = v[:, pair * LANE:(pair + 1) * LANE]
        v_ref[:, (2 * pair + 1) * LANE:(2 * pair + 2) * LANE] = ones


def _layer_block(stack, layer):
    return pl.BlockSpec((None,) + stack.shape[1:], lambda *_: (layer, 0, 0))


def _mla_proj(xb, wa, gq, gkv, wuq, wuqp, wuk, wuv, cos, sin, layer, tm=512):
    t = xb.shape[0]
    na = wa.shape[2]
    nq = MLA_HEADS * LANE
    nv = MLA_HEADS * LANE
    rows = lambda n: pl.BlockSpec((tm, n), lambda i: (i, 0))
    stacks = (wa, gq, gkv, wuq, wuqp, wuk, wuv)
    vmem = _vmem_estimate(
        [((tm, D_MODEL), BF16)] + [(a.shape[1:], BF16) for a in (wa, wuq, wuqp, wuk, wuv)]
        + [((tm, LANE), F32), ((tm, LANE), F32), ((tm, nq), BF16), ((tm, nq), BF16), ((tm, nv), BF16)],
        [((tm, na), F32)] + [((tm, nq), F32)] * 4)
    return pl.pallas_call(
        functools.partial(_mla_proj_kernel, scale=LOG2E * (MLA_NOPE + MLA_ROPE) ** -0.5),
        grid=(t // tm,),
        in_specs=[rows(D_MODEL)] + [_layer_block(a, layer) for a in stacks] + [rows(LANE), rows(LANE)],
        out_specs=[rows(nq), rows(nq), rows(nv)],
        out_shape=[jax.ShapeDtypeStruct((t, nq), BF16), jax.ShapeDtypeStruct((t, nq), BF16),
                   jax.ShapeDtypeStruct((t, nv), BF16)],
        compiler_params=_compiler_params(("parallel",), vmem),
        name="mla_proj",
    )(xb, *stacks, cos, sin)


def _branch_proj_kernel(x_ref, w_ref, cos_ref, sin_ref, qd_ref, kd_ref, vd_ref, qe_ref, *, q_scale, mem_scale):
    tm = x_ref.shape[0]
    x = x_ref[...]
    cos = cos_ref[...]
    sin = sin_ref[...]
    x1 = _x1_mask_diff(tm)

    def rotated(y, scale, o_ref):
        for hd in range(DIFF_HEADS):
            sl = slice(hd * LANE, (hd + 1) * LANE)
            yh = _rope_rotate(y[:, sl], cos, sin, x1, DIFF_ROPE_HALF)
            o_ref[:, sl] = (yh if scale == 1.0 else yh * scale).astype(o_ref.dtype)

    rotated(_dot(x, w_ref[:, :DIFF_QK]), q_scale, qd_ref)
    rotated(_dot(x, w_ref[:, DIFF_QK:2 * DIFF_QK]), 1.0, kd_ref)
    v = _dot(x, w_ref[:, 2 * DIFF_QK:2 * DIFF_QK + DIFF_OUT]).astype(vd_ref.dtype)
    ones = jnp.ones((tm, LANE), vd_ref.dtype)
    for hd in range(DIFF_HEADS):
        vd_ref[:, 2 * hd * LANE:(2 * hd + 1) * LANE] = v[:, hd * LANE:(hd + 1) * LANE]
        vd_ref[:, (2 * hd + 1) * LANE:(2 * hd + 2) * LANE] = ones
    qe_ref[...] = (_dot(x, w_ref[:, 2 * DIFF_QK + DIFF_OUT:]) * mem_scale).astype(qe_ref.dtype)


def _branch_proj(xb, w_stack, cos, sin, layer, tm=512):
    t, kdim = xb.shape
    n = w_stack.shape[2]
    rows = lambda width: pl.BlockSpec((tm, width), lambda i: (i, 0))
    widths = (DIFF_QK, DIFF_QK, 2 * DIFF_OUT, MEM_OUT)
    vmem = _vmem_estimate(
        [((tm, kdim), BF16), ((kdim, n), BF16), ((tm, LANE), F32), ((tm, LANE), F32)]
        + [((tm, width), BF16) for width in widths],
        [((tm, DIFF_QK), F32)] * 6)
    return pl.pallas_call(
        functools.partial(_branch_proj_kernel, q_scale=LOG2E * DIFF_D ** -0.5, mem_scale=MEM_HD ** -0.5),
        grid=(t // tm,),
        in_specs=[rows(kdim), _layer_block(w_stack, layer), rows(LANE), rows(LANE)],
        out_specs=[rows(width) for width in widths],
        out_shape=[jax.ShapeDtypeStruct((t, width), BF16) for width in widths],
        compiler_params=_compiler_params(("parallel",), vmem),
        name="branch_proj",
    )(xb, w_stack, cos, sin)


def _mem_kv_kernel(x_ref, w_ref, o_ref):
    o_ref[...] = _dot(x_ref[...], w_ref[...]).astype(o_ref.dtype)


def _mem_kv_proj(memb, w_stack, layer, tm):
    t, kdim = memb.shape
    n = w_stack.shape[2]
    vmem = _vmem_estimate([((tm, kdim), BF16), ((kdim, n), BF16), ((tm, n), BF16)], [((tm, n), F32)] * 2)
    return pl.pallas_call(
        _mem_kv_kernel,
        grid=(t // tm,),
        in_specs=[pl.BlockSpec((tm, kdim), lambda i: (i, 0)),
                  _layer_block(w_stack, layer)],
        out_specs=pl.BlockSpec((tm, n), lambda i: (i, 0)),
        out_shape=jax.ShapeDtypeStruct((t, n), BF16),
        compiler_params=_compiler_params(("parallel",), vmem),
        name="mem_kv_proj",
    )(memb, w_stack)


def _softmax_pv(q, k, v_ext):
    s = _dot_nt(q, k)
    m = jnp.max(s, axis=-1, keepdims=True)
    p = jnp.exp2((s - m).astype(BF16))
    pv = _dot(p, v_ext)
    return pv[:, :LANE] / pv[:, LANE:]


def _mla_attn_kernel(q_ref, k_ref, v_ref, o_ref, *, pairs):
    tq = q_ref.shape[0]
    first = lax.broadcasted_iota(jnp.int32, (tq, LANE), 1) < MLA_V
    for pr in range(pairs):
        v_ext = v_ref[:, 2 * pr * LANE:(2 * pr + 2) * LANE]
        outs = []
        for j in range(2):
            sl = slice((2 * pr + j) * LANE, (2 * pr + j + 1) * LANE)
            outs.append(_softmax_pv(q_ref[:, sl], k_ref[:, sl], v_ext))
        o_ref[:, pr * LANE:(pr + 1) * LANE] = jnp.where(first, outs[0], outs[1]).astype(o_ref.dtype)


def _mla_attn(q, k, v, batch, seq, tq=512, pairs=4):
    t = q.shape[0]
    nq = seq // tq
    wide = 2 * pairs * LANE
    grid = (batch, MLA_HEADS // (2 * pairs), nq)
    vmem = _vmem_estimate(
        [((tq, wide), BF16), ((seq, wide), BF16), ((seq, wide), BF16), ((tq, pairs * LANE), BF16)],
        ([((tq, seq), F32)] * 2 + [((tq, seq), BF16)] * 2) * 2 * pairs)
    return pl.pallas_call(
        functools.partial(_mla_attn_kernel, pairs=pairs),
        grid=grid,
        in_specs=[
            pl.BlockSpec((tq, wide), lambda b, g, i: (b * nq + i, g)),
            pl.BlockSpec((seq, wide), lambda b, g, i: (b, g)),
            pl.BlockSpec((seq, wide), lambda b, g, i: (b, g)),
        ],
        out_specs=pl.BlockSpec((tq, pairs * LANE), lambda b, g, i: (b * nq + i, g)),
        out_shape=jax.ShapeDtypeStruct((t, MLA_OUT), BF16),
        compiler_params=_compiler_params(("parallel", "parallel", "arbitrary"), vmem),
        name="mla_attn",
    )(q, k, v)


def _diff_attn_kernel(q_ref, k_ref, v_ref, lam_ref, g_ref, o_ref, *, lambda_init, heads):
    tq = q_ref.shape[0]
    lam = lam_ref[...]
    lam_a = jnp.sum(lam[0:1] * lam[1:2], axis=-1, keepdims=True)
    lam_b = jnp.sum(lam[2:3] * lam[3:4], axis=-1, keepdims=True)
    lambda_full = jnp.exp(lam_a) - jnp.exp(lam_b) + lambda_init
    map0 = lax.broadcasted_iota(jnp.int32, (tq, LANE), 1) < DIFF_D
    gain = g_ref[...] * (1.0 - lambda_init)
    for hd in range(heads):
        sl = slice(hd * LANE, (hd + 1) * LANE)
        q = q_ref[:, sl]
        k = k_ref[:, sl]
        v_ext = v_ref[:, 2 * hd * LANE:(2 * hd + 2) * LANE]
        zero = jnp.zeros_like(q)
        o = _softmax_pv(jnp.where(map0, q, zero), k, v_ext) - lambda_full * _softmax_pv(
            jnp.where(map0, zero, q), k, v_ext)
        o = o * lax.rsqrt(jnp.mean(o * o, axis=-1, keepdims=True) + 1e-5) * gain
        o_ref[:, sl] = o.astype(o_ref.dtype)


def _diff_attn(q, k, v, lam_stack, subln_stack, layer, lambda_init, batch, seq, tq=256, heads=8):
    t = q.shape[0]
    nq = seq // tq
    wide = heads * LANE
    vmem = _vmem_estimate(
        [((tq, wide), BF16), ((seq, wide), BF16), ((seq, 2 * wide), BF16), ((tq, wide), BF16)],
        ([((tq, seq), F32)] * 2 + [((tq, seq), BF16)] * 2) * 2 * heads)
    return pl.pallas_call(
        functools.partial(_diff_attn_kernel, lambda_init=lambda_init, heads=heads),
        grid=(batch, DIFF_HEADS // heads, nq),
        in_specs=[
            pl.BlockSpec((tq, wide), lambda b, h, i: (b * nq + i, h)),
            pl.BlockSpec((seq, wide), lambda b, h, i: (b, h)),
            pl.BlockSpec((seq, 2 * wide), lambda b, h, i: (b, h)),
            _layer_block(lam_stack, layer),
            _layer_block(subln_stack, layer),
        ],
        out_specs=pl.BlockSpec((tq, wide), lambda b, h, i: (b * nq + i, h)),
        out_shape=jax.ShapeDtypeStruct((t, DIFF_OUT), BF16),
        compiler_params=_compiler_params(("parallel", "parallel", "arbitrary"), vmem),
        name="diff_attn",
    )(q, k, v, lam_stack, subln_stack)


def _mem_attn_kernel(q_ref, kv_ref, o_ref):
    for hd in range(MEM_HEADS):
        sl = slice(hd * MEM_HD, (hd + 1) * MEM_HD)
        vsl = slice(MEM_OUT + hd * MEM_HD, MEM_OUT + (hd + 1) * MEM_HD)
        s = _dot_nt(q_ref[:, sl], kv_ref[:, sl])
        p = jnp.exp(s - jnp.max(s, axis=-1, keepdims=True))
        inv_l = 1.0 / jnp.sum(p, axis=-1, keepdims=True)
        o_ref[:, sl] = (_dot(p.astype(BF16), kv_ref[:, vsl]) * inv_l).astype(o_ref.dtype)


def _mem_attn(q, kv, batch, seq, mem_len, tq=1024):
    t = q.shape[0]
    nq = seq // tq
    vmem = _vmem_estimate(
        [((tq, MEM_OUT), BF16), ((mem_len, 2 * MEM_OUT), BF16), ((tq, MEM_OUT), BF16)],
        [((tq, mem_len), F32)] * 3 + [((tq, MEM_OUT), F32)])
    return pl.pallas_call(
        _mem_attn_kernel,
        grid=(batch, nq),
        in_specs=[
            pl.BlockSpec((tq, MEM_OUT), lambda b, i: (b * nq + i, 0)),
            pl.BlockSpec((mem_len, 2 * MEM_OUT), lambda b, i: (b, 0)),
        ],
        out_specs=pl.BlockSpec((tq, MEM_OUT), lambda b, i: (b * nq + i, 0)),
        out_shape=jax.ShapeDtypeStruct((t, MEM_OUT), BF16),
        compiler_params=_compiler_params(("parallel", "arbitrary"), vmem),
        name="mem_attn",
    )(q, kv)


MERGE_CHUNKS = 2


def _merge_kernel(x_ref, xb_ref, omla_ref, odiff_ref, omem_ref, wg_ref, bg_ref, wb_ref, wo_ref, g_ref, b_ref,
                  y_ref, yb_ref):
    chunk = x_ref.shape[0] // MERGE_CHUNKS
    for c in range(MERGE_CHUNKS):
        rows = slice(c * chunk, (c + 1) * chunk)
        xb = xb_ref[rows, :]
        merged = None
        row = 0
        for i, o_ref in enumerate((omla_ref, odiff_ref, omem_ref)):
            sl = slice(i * D_MODEL, (i + 1) * D_MODEL)
            width = o_ref.shape[1]
            gate = jax.nn.sigmoid(_dot(xb, wg_ref[:, sl]) + bg_ref[i:i + 1, :])
            term = gate * _dot(o_ref[rows, :], wb_ref[row:row + width, :])
            merged = term if merged is None else merged + term
            row += width
        z = DEEPNORM_ALPHA * x_ref[rows, :] + _dot(merged.astype(BF16), wo_ref[...])
        y = _layernorm(z, g_ref[...], b_ref[...])
        y_ref[rows, :] = y
        yb_ref[rows, :] = y.astype(BF16)


def _merge(x, xb, omla, odiff, omem, wg_stack, bg_stack, wb_stack, wo_stack, g_stack, b_stack, layer, tm=512):
    t = x.shape[0]
    rows = lambda n: pl.BlockSpec((tm, n), lambda i: (i, 0))
    vmem = _vmem_estimate(
        [((tm, D_MODEL), F32), ((tm, D_MODEL), BF16), ((tm, MLA_OUT), BF16), ((tm, DIFF_OUT), BF16),
         ((tm, MEM_OUT), BF16), (wg_stack.shape[1:], BF16), (wb_stack.shape[1:], BF16), (wo_stack.shape[1:], BF16),
         ((tm, D_MODEL), F32), ((tm, D_MODEL), BF16)],
        [((tm, D_MODEL), F32)] * 5)
    return pl.pallas_call(
        _merge_kernel,
        grid=(t // tm,),
        in_specs=[rows(D_MODEL), rows(D_MODEL), rows(MLA_OUT), rows(DIFF_OUT), rows(MEM_OUT),
                  _layer_block(wg_stack, layer), _layer_block(bg_stack, layer),
                  _layer_block(wb_stack, layer), _layer_block(wo_stack, layer),
                  _layer_block(g_stack, layer), _layer_block(b_stack, layer)],
        out_specs=[rows(D_MODEL), rows(D_MODEL)],
        out_shape=[jax.ShapeDtypeStruct((t, D_MODEL), F32), jax.ShapeDtypeStruct((t, D_MODEL), BF16)],
        compiler_params=_compiler_params(("parallel",), vmem),
        name="merge",
    )(x, xb, omla, odiff, omem, wg_stack, bg_stack, wb_stack, wo_stack, g_stack, b_stack)


MLP_EPILOGUE_CHUNKS = 4


def _mlp_kernel(x_ref, xb_ref, w1_ref, w2_ref, g_ref, b_ref, y_ref, yb_ref, acc_ref):
    kf = pl.program_id(1)

    def hidden():
        h = jnp.maximum(_dot(xb_ref[...], w1_ref[...]), 0.0)
        return (h * h).astype(BF16)

    last = pl.num_programs(1) - 1

    @pl.when(kf == 0)
    def _():
        acc_ref[...] = _dot(hidden(), w2_ref[...])

    @pl.when((kf > 0) & (kf < last))
    def _():
        acc_ref[...] += _dot(hidden(), w2_ref[...])

    @pl.when(kf == last)
    def _():
        hh = hidden()
        chunk = x_ref.shape[0] // MLP_EPILOGUE_CHUNKS
        for c in range(MLP_EPILOGUE_CHUNKS):
            rows = slice(c * chunk, (c + 1) * chunk)
            z = DEEPNORM_ALPHA * x_ref[rows, :] + (acc_ref[rows, :] + _dot(hh[rows, :], w2_ref[...]))
            y = _layernorm(z, g_ref[...], b_ref[...])
            y_ref[rows, :] = y
            yb_ref[rows, :] = y.astype(BF16)


def _mlp(x, xb, w1_stack, w2_stack, g_stack, b_stack, layer, tm=1024, tf=1024):
    t = x.shape[0]
    vmem = _vmem_estimate(
        [((tm, D_MODEL), F32), ((tm, D_MODEL), BF16), ((D_MODEL, tf), BF16), ((tf, D_MODEL), BF16),
         ((tm, D_MODEL), F32), ((tm, D_MODEL), BF16)],
        [((tm, D_MODEL), F32), ((tm, tf), F32), ((tm, tf), F32), ((tm, tf), BF16), ((tm, D_MODEL), F32)])
    return pl.pallas_call(
        _mlp_kernel,
        grid=(t // tm, D_FF // tf),
        in_specs=[
            pl.BlockSpec((tm, D_MODEL), lambda i, k: (i, 0)),
            pl.BlockSpec((tm, D_MODEL), lambda i, k: (i, 0)),
            pl.BlockSpec((None, D_MODEL, tf), lambda i, k: (layer, 0, k)),
            pl.BlockSpec((None, tf, D_MODEL), lambda i, k: (layer, k, 0)),
            _layer_block(g_stack, layer),
            _layer_block(b_stack, layer),
        ],
        out_specs=[pl.BlockSpec((tm, D_MODEL), lambda i, k: (i, 0))] * 2,
        out_shape=[jax.ShapeDtypeStruct((t, D_MODEL), F32), jax.ShapeDtypeStruct((t, D_MODEL), BF16)],
        scratch_shapes=[pltpu.VMEM((tm, D_MODEL), F32)],
        compiler_params=_compiler_params(("parallel", "arbitrary"), vmem),
        name="mlp",
    )(x, xb, w1_stack, w2_stack, g_stack, b_stack)


def _lane_patterns():
    lane = jnp.arange(LANE)
    inv_m = ROPE_THETA ** (-jnp.arange(0, MLA_ROPE, 2, dtype=F32) / MLA_ROPE)
    rel = lane - MLA_ROPE_LO
    in_rope = (rel >= 0) & (rel < MLA_ROPE)
    inv_mla = jnp.where(in_rope, inv_m[jnp.clip(rel, 0, MLA_ROPE - 1) % MLA_ROPE_HALF], 0.0)
    sgn_mla = jnp.where(in_rope, jnp.where(rel < MLA_ROPE_HALF, -1.0, 1.0), 0.0)
    inv_d = ROPE_THETA ** (-jnp.arange(0, DIFF_ROT, 2, dtype=F32) / DIFF_ROT)
    rel = lane % DIFF_D
    in_rope = rel < DIFF_ROT
    inv_diff = jnp.where(in_rope, inv_d[rel % DIFF_ROPE_HALF], 0.0)
    sgn_diff = jnp.where(in_rope, jnp.where(rel < DIFF_ROPE_HALF, -1.0, 1.0), 0.0)
    row = lambda a: a.astype(F32).reshape(1, LANE)
    return row(inv_mla), row(sgn_mla), row(inv_diff), row(sgn_diff)


def _weight_stacks(w_in, mla_w_uq, mla_w_ukv):
    depth, d, _ = w_in.shape
    lat = MLA_Q_LORA + MLA_KV_LORA
    branch_lo = lat + MLA_ROPE
    gate_lo = branch_lo + 2 * DIFF_QK + DIFF_OUT + MEM_OUT
    w_a = jnp.concatenate(
        [w_in[:, :, :lat], jnp.zeros((depth, d, MLA_ROPE_LO), F32), w_in[:, :, lat:branch_lo],
         jnp.zeros((depth, d, LANE - MLA_ROPE_LO - MLA_ROPE), F32)], axis=2).astype(BF16)
    w_branch_in = w_in[:, :, branch_lo:gate_lo].astype(BF16)
    w_gate = w_in[:, :, gate_lo:].astype(BF16)

    qd = MLA_NOPE + MLA_ROPE
    w_uq = jnp.pad(mla_w_uq.reshape(depth, MLA_Q_LORA, MLA_HEADS, qd), ((0, 0), (0, 0), (0, 0), (0, LANE - qd)))
    lane = jnp.arange(LANE)
    rel = lane - MLA_ROPE_LO
    in_rope = (rel >= 0) & (rel < MLA_ROPE)
    partner = jnp.where(rel < MLA_ROPE_HALF, lane + MLA_ROPE_HALF, lane - MLA_ROPE_HALF)
    w_uq_partner = jnp.where(in_rope, jnp.take(w_uq, jnp.clip(partner, 0, LANE - 1), axis=3), 0.0)
    flat = lambda a: a.reshape(depth, a.shape[1], -1).astype(BF16)
    ukv = mla_w_ukv.reshape(depth, MLA_KV_LORA, MLA_HEADS, MLA_NOPE + MLA_V)
    w_uk = jnp.pad(ukv[..., :MLA_NOPE], ((0, 0), (0, 0), (0, 0), (0, LANE - MLA_NOPE)))
    return w_a, w_branch_in, w_gate, flat(w_uq), flat(w_uq_partner), flat(w_uk), flat(ukv[..., MLA_NOPE:])


def kernel(x, mem, positions, w_in, b_gate, mla_q_norm, mla_kv_norm, mla_w_uq, mla_w_ukv, diff_lambda, diff_subln,
           mem_w_kv, w_branch, w_out, ln1_g, ln1_b, mlp_w1, mlp_w2, ln2_g, ln2_b):
    batch, seq, d = x.shape
    mem_len = mem.shape[1]
    t = batch * seq
    depth = w_in.shape[0]

    inv_mla, sgn_mla, inv_diff, sgn_diff = _lane_patterns()
    pos = positions.reshape(t, 1)
    cos_m, sin_m = _rope_tables(pos, inv_mla, sgn_mla)
    cos_d, sin_d = _rope_tables(pos, inv_diff, sgn_diff)

    xf = x.reshape(t, d)
    xb = xf.astype(BF16)
    memb = mem.reshape(batch * mem_len, d).astype(BF16)

    w_a, w_branch_in, w_gate, w_uq, w_uq_partner, w_uk, w_uv = _weight_stacks(w_in, mla_w_uq, mla_w_ukv)
    mem_w_kv_b, w_branch_b, w_out_b, w1_b, w2_b = (
        a.astype(BF16) for a in (mem_w_kv, w_branch, w_out, mlp_w1, mlp_w2))
    stack_rows = lambda a: a.reshape(depth, 1, -1)
    q_norm_s, kv_norm_s, subln_s, ln1_g_s, ln1_b_s, ln2_g_s, ln2_b_s = map(
        stack_rows, (mla_q_norm, mla_kv_norm, diff_subln, ln1_g, ln1_b, ln2_g, ln2_b))

    for l in range(depth):
        lambda_init = 0.8 - 0.6 * math.exp(-0.3 * l)
        q_m, k_m, v_m = _mla_proj(xb, w_a, q_norm_s, kv_norm_s, w_uq, w_uq_partner, w_uk, w_uv, cos_m, sin_m, l)
        o_mla = _mla_attn(q_m, k_m, v_m, batch, seq)

        q_d, k_d, v_d, q_e = _branch_proj(xb, w_branch_in, cos_d, sin_d, l)
        o_diff = _diff_attn(q_d, k_d, v_d, diff_lambda, subln_s, l, lambda_init, batch, seq)

        kv_e = _mem_kv_proj(memb, mem_w_kv_b, l, tm=mem_len)
        o_mem = _mem_attn(q_e, kv_e, batch, seq, mem_len)

        xf, xb = _merge(xf, xb, o_mla, o_diff, o_mem, w_gate, b_gate, w_branch_b, w_out_b, ln1_g_s, ln1_b_s, l)
        xf, xb = _mlp(xf, xb, w1_b, w2_b, ln2_g_s, ln2_b_s, l)
    return xf.reshape(batch, seq, d)
```

```python
import functools
import math

import jax
import jax.numpy as jnp
from jax import lax
from jax.experimental import pallas as pl
from jax.experimental.pallas import tpu as pltpu

D_MODEL = 1024
DEPTH = 4
ROPE_THETA = 500000.0
MLA_HEADS = 8
MLA_NOPE = 64
MLA_ROPE = 32
MLA_V = 64
MLA_Q_LORA = 384
MLA_KV_LORA = 256
DIFF_HEADS = 8
DIFF_D = 64
DIFF_ROT = DIFF_D // 4
MEM_HEADS = 4
MEM_HD = 128
N_BRANCH = 3
D_FF = 4 * D_MODEL
DEEPNORM_ALPHA = (2 * DEPTH) ** 0.25
LN_EPS = 1e-5
MLA_OUT = MLA_HEADS * MLA_V
DIFF_QK = DIFF_HEADS * 2 * DIFF_D
DIFF_OUT = DIFF_HEADS * 2 * DIFF_D
MEM_OUT = MEM_HEADS * MEM_HD

LANE = 128
V7X_VMEM_BYTES = 64 * 1024 * 1024

MLA_ROPE_LO = MLA_NOPE
MLA_ROPE_HALF = MLA_ROPE // 2
DIFF_ROPE_HALF = DIFF_ROT // 2

BF16 = jnp.bfloat16
F32 = jnp.float32
LOG2E = math.log2(math.e)


def _dot(a, b):
    return jnp.dot(a, b, preferred_element_type=F32)


def _dot_nt(a, b):
    return lax.dot_general(a, b, (((1,), (1,)), ((), ())), preferred_element_type=F32)


def _compiler_params(semantics, vmem_bytes):
    limit = min(int(vmem_bytes), V7X_VMEM_BYTES - 8 * 1024 * 1024)
    return pltpu.CompilerParams(dimension_semantics=semantics, vmem_limit_bytes=limit)


def _nbytes(shape, dtype):
    return math.prod(shape) * jnp.dtype(dtype).itemsize


def _vmem_estimate(blocks, temps):
    return 2 * sum(_nbytes(s, d) for s, d in blocks) + sum(_nbytes(s, d) for s, d in temps) + (4 << 20)


def _rope_rotate(y, cos, sin_signed, x1_mask, half):
    partner = jnp.where(x1_mask, pltpu.roll(y, LANE - half, 1), pltpu.roll(y, half, 1))
    return y * cos + partner * sin_signed


def _x1_mask_mla(rows):
    lane = lax.broadcasted_iota(jnp.int32, (rows, LANE), 1)
    return (lane >= MLA_ROPE_LO) & (lane < MLA_ROPE_LO + MLA_ROPE_HALF)


def _x1_mask_diff(rows):
    lane = lax.broadcasted_iota(jnp.int32, (rows, LANE), 1) % DIFF_D
    return lane < DIFF_ROPE_HALF


def _layernorm(z, g, b):
    mu = jnp.mean(z, axis=-1, keepdims=True)
    zc = z - mu
    var = jnp.mean(zc * zc, axis=-1, keepdims=True)
    return zc * lax.rsqrt(var + LN_EPS) * g + b


def _rope_table_kernel(pos_ref, inv_ref, sgn_ref, cos_ref, sin_ref):
    ang = pos_ref[...].astype(F32) * inv_ref[...]
    cos_ref[...] = jnp.cos(ang)
    sin_ref[...] = jnp.sin(ang) * sgn_ref[...]


def _rope_tables(pos, inv_lane, sgn_lane, tm=2048):
    t = pos.shape[0]
    return pl.pallas_call(
        _rope_table_kernel,
        grid=(t // tm,),
        in_specs=[
            pl.BlockSpec((tm, 1), lambda i: (i, 0)),
            pl.BlockSpec((1, LANE), lambda i: (0, 0)),
            pl.BlockSpec((1, LANE), lambda i: (0, 0)),
        ],
        out_specs=[pl.BlockSpec((tm, LANE), lambda i: (i, 0))] * 2,
        out_shape=[jax.ShapeDtypeStruct((t, LANE), F32)] * 2,
        compiler_params=_compiler_params(("parallel",), 32 << 20),
        name="rope_tables",
    )(pos, inv_lane, sgn_lane)


def _mla_proj_kernel(x_ref, wa_ref, gq_ref, gkv_ref, wuq_ref, wuqp_ref, wuk_ref, wuv_ref, cos_ref, sin_ref,
                     q_ref, k_ref, v_ref, *, scale):
    tm = x_ref.shape[0]
    h = _dot(x_ref[...], wa_ref[...])
    cq = h[:, :MLA_Q_LORA]
    ckv = h[:, MLA_Q_LORA:MLA_Q_LORA + MLA_KV_LORA]
    kpe = h[:, MLA_Q_LORA + MLA_KV_LORA:]

    def rms(v, g):
        return (v * lax.rsqrt(jnp.mean(v * v, axis=-1, keepdims=True) + 1e-6) * g).astype(BF16)

    cqn = rms(cq, gq_ref[...])
    ckvn = rms(ckv, gkv_ref[...])
    cos = cos_ref[...]
    sin = sin_ref[...]
    kpe_rot = _rope_rotate(kpe, cos, sin, _x1_mask_mla(tm), MLA_ROPE_HALF)

    q = _dot(cqn, wuq_ref[...])
    q_partner = _dot(cqn, wuqp_ref[...])
    kn = _dot(ckvn, wuk_ref[...])
    cos_q = cos * scale
    sin_q = sin * scale
    for hd in range(MLA_HEADS):
        sl = slice(hd * LANE, (hd + 1) * LANE)
        q_ref[:, sl] = (q[:, sl] * cos_q + q_partner[:, sl] * sin_q).astype(BF16)
        k_ref[:, sl] = (kn[:, sl] + kpe_rot).astype(BF16)
    v = _dot(ckvn, wuv_ref[...]).astype(BF16)
    ones = jnp.ones((tm, LANE), BF16)
    for pair in range(MLA_HEADS // 2):
        v_ref[:, 2 * pair * LANE:(2 * pair + 1) * LANE] = v[:, pair * LANE:(pair + 1) * LANE]
        v_ref[:, (2 * pair + 1) * LANE:(2 * pair + 2) * LANE] = ones


def _layer_block(stack, layer):
    return pl.BlockSpec((None,) + stack.shape[1:], lambda *_: (layer, 0, 0))


def _mla_proj(xb, wa, gq, gkv, wuq, wuqp, wuk, wuv, cos, sin, layer, tm=512):
    t = xb.shape[0]
    na = wa.shape[2]
    nq = MLA_HEADS * LANE
    nv = MLA_HEADS * LANE
    rows = lambda n: pl.BlockSpec((tm, n), lambda i: (i, 0))
    stacks = (wa, gq, gkv, wuq, wuqp, wuk, wuv)
    vmem = _vmem_estimate(
        [((tm, D_MODEL), BF16)] + [(a.shape[1:], BF16) for a in (wa, wuq, wuqp, wuk, wuv)]
        + [((tm, LANE), F32), ((tm, LANE), F32), ((tm, nq), BF16), ((tm, nq), BF16), ((tm, nv), BF16)],
        [((tm, na), F32)] + [((tm, nq), F32)] * 4)
    return pl.pallas_call(
        functools.partial(_mla_proj_kernel, scale=LOG2E * (MLA_NOPE + MLA_ROPE) ** -0.5),
        grid=(t // tm,),
        in_specs=[rows(D_MODEL)] + [_layer_block(a, layer) for a in stacks] + [rows(LANE), rows(LANE)],
        out_specs=[rows(nq), rows(nq), rows(nv)],
        out_shape=[jax.ShapeDtypeStruct((t, nq), BF16), jax.ShapeDtypeStruct((t, nq), BF16),
                   jax.ShapeDtypeStruct((t, nv), BF16)],
        compiler_params=_compiler_params(("parallel",), vmem),
        name="mla_proj",
    )(xb, *stacks, cos, sin)


def _branch_proj_kernel(x_ref, w_ref, cos_ref, sin_ref, qd_ref, kd_ref, vd_ref, qe_ref, *, q_scale, mem_scale):
    tm = x_ref.shape[0]
    x = x_ref[...]
    cos = cos_ref[...]
    sin = sin_ref[...]
    x1 = _x1_mask_diff(tm)

    def rotated(y, scale, o_ref):
        for hd in range(DIFF_HEADS):
            sl = slice(hd * LANE, (hd + 1) * LANE)
            yh = _rope_rotate(y[:, sl], cos, sin, x1, DIFF_ROPE_HALF)
            o_ref[:, sl] = (yh if scale == 1.0 else yh * scale).astype(o_ref.dtype)

    rotated(_dot(x, w_ref[:, :DIFF_QK]), q_scale, qd_ref)
    rotated(_dot(x, w_ref[:, DIFF_QK:2 * DIFF_QK]), 1.0, kd_ref)
    v = _dot(x, w_ref[:, 2 * DIFF_QK:2 * DIFF_QK + DIFF_OUT]).astype(vd_ref.dtype)
    ones = jnp.ones((tm, LANE), vd_ref.dtype)
    for hd in range(DIFF_HEADS):
        vd_ref[:, 2 * hd * LANE:(2 * hd + 1) * LANE] = v[:, hd * LANE:(hd + 1) * LANE]
        vd_ref[:, (2 * hd + 1) * LANE:(2 * hd + 2) * LANE] = ones
    qe_ref[...] = (_dot(x, w_ref[:, 2 * DIFF_QK + DIFF_OUT:]) * mem_scale).astype(qe_ref.dtype)


def _branch_proj(xb, w_stack, cos, sin, layer, tm=512):
    t, kdim = xb.shape
    n = w_stack.shape[2]
    rows = lambda width: pl.BlockSpec((tm, width), lambda i: (i, 0))
    widths = (DIFF_QK, DIFF_QK, 2 * DIFF_OUT, MEM_OUT)
    vmem = _vmem_estimate(
        [((tm, kdim), BF16), ((kdim, n), BF16), ((tm, LANE), F32), ((tm, LANE), F32)]
        + [((tm, width), BF16) for width in widths],
        [((tm, DIFF_QK), F32)] * 6)
    return pl.pallas_call(
        functools.partial(_branch_proj_kernel, q_scale=LOG2E * DIFF_D ** -0.5, mem_scale=MEM_HD ** -0.5),
        grid=(t // tm,),
        in_specs=[rows(kdim), _layer_block(w_stack, layer), rows(LANE), rows(LANE)],
        out_specs=[rows(width) for width in widths],
        out_shape=[jax.ShapeDtypeStruct((t, width), BF16) for width in widths],
        compiler_params=_compiler_params(("parallel",), vmem),
        name="branch_proj",
    )(xb, w_stack, cos, sin)


def _mem_kv_kernel(x_ref, w_ref, o_ref):
    o_ref[...] = _dot(x_ref[...], w_ref[...]).astype(o_ref.dtype)


def _mem_kv_proj(memb, w_stack, layer, tm):
    t, kdim = memb.shape
    n = w_stack.shape[2]
    vmem = _vmem_estimate([((tm, kdim), BF16), ((kdim, n), BF16), ((tm, n), BF16)], [((tm, n), F32)] * 2)
    return pl.pallas_call(
        _mem_kv_kernel,
        grid=(t // tm,),
        in_specs=[pl.BlockSpec((tm, kdim), lambda i: (i, 0)),
                  _layer_block(w_stack, layer)],
        out_specs=pl.BlockSpec((tm, n), lambda i: (i, 0)),
        out_shape=jax.ShapeDtypeStruct((t, n), BF16),
        compiler_params=_compiler_params(("parallel",), vmem),
        name="mem_kv_proj",
    )(memb, w_stack)


def _softmax_pv(q, k, v_ext):
    return _normalized_pv(_softmax_numerators(q, k), v_ext)


def _softmax_numerators(q, k):
    return _numerators_from_scores(_dot_nt(q, k))


def _numerators_from_scores(s):
    m = jnp.max(s, axis=-1, keepdims=True)
    return jnp.exp2((s - m).astype(BF16))


def _normalized_pv(p, v_ext):
    pv = _dot(p, v_ext)
    return pv[:, :LANE] / pv[:, LANE:]


ATTN_ROWS = 256


def _mla_attn_kernel(q_ref, k_ref, v_ref, o_ref, *, pairs):
    first = lax.broadcasted_iota(jnp.int32, (ATTN_ROWS, LANE), 1) < MLA_V
    for sub in range(q_ref.shape[0] // ATTN_ROWS):
        rows = slice(sub * ATTN_ROWS, (sub + 1) * ATTN_ROWS)
        for pr in range(pairs):
            v_ext = v_ref[:, 2 * pr * LANE:(2 * pr + 2) * LANE]
            outs = []
            for j in range(2):
                sl = slice((2 * pr + j) * LANE, (2 * pr + j + 1) * LANE)
                outs.append(_softmax_pv(q_ref[rows, sl], k_ref[:, sl], v_ext))
            o_ref[rows, pr * LANE:(pr + 1) * LANE] = jnp.where(first, outs[0], outs[1]).astype(o_ref.dtype)


def _mla_attn(q, k, v, batch, seq, tq=512, pairs=4):
    t = q.shape[0]
    nq = seq // tq
    wide = 2 * pairs * LANE
    grid = (batch, MLA_HEADS // (2 * pairs), nq)
    vmem = _vmem_estimate(
        [((tq, wide), BF16), ((seq, wide), BF16), ((seq, wide), BF16), ((tq, pairs * LANE), BF16)],
        ([((tq, seq), F32)] * 2 + [((tq, seq), BF16)] * 2) * 2 * pairs)
    return pl.pallas_call(
        functools.partial(_mla_attn_kernel, pairs=pairs),
        grid=grid,
        in_specs=[
            pl.BlockSpec((tq, wide), lambda b, g, i: (b * nq + i, g)),
            pl.BlockSpec((seq, wide), lambda b, g, i: (b, g)),
            pl.BlockSpec((seq, wide), lambda b, g, i: (b, g)),
        ],
        out_specs=pl.BlockSpec((tq, pairs * LANE), lambda b, g, i: (b * nq + i, g)),
        out_shape=jax.ShapeDtypeStruct((t, MLA_OUT), BF16),
        compiler_params=_compiler_params(("parallel", "parallel", "arbitrary"), vmem),
        name="mla_attn",
    )(q, k, v)


def _diff_attn_kernel(q_ref, k_ref, v_ref, lam_ref, g_ref, o_ref, *, lambda_init, heads):
    tq = q_ref.shape[0]
    lam = lam_ref[...]
    lam_a = jnp.sum(lam[0:1] * lam[1:2], axis=-1, keepdims=True)
    lam_b = jnp.sum(lam[2:3] * lam[3:4], axis=-1, keepdims=True)
    lambda_full = jnp.exp(lam_a) - jnp.exp(lam_b) + lambda_init
    map0 = lax.broadcasted_iota(jnp.int32, (ATTN_ROWS, LANE), 1) < DIFF_D
    gain = g_ref[...] * (1.0 - lambda_init)
    for sub in range(tq // ATTN_ROWS):
        rows = slice(sub * ATTN_ROWS, (sub + 1) * ATTN_ROWS)
        for hd in range(heads):
            sl = slice(hd * LANE, (hd + 1) * LANE)
            q = q_ref[rows, sl]
            k = k_ref[:, sl]
            v_ext = v_ref[:, 2 * hd * LANE:(2 * hd + 2) * LANE]
            zero = jnp.zeros_like(q)
            o = _softmax_pv(jnp.where(map0, q, zero), k, v_ext) - lambda_full * _softmax_pv(
                jnp.where(map0, zero, q), k, v_ext)
            o = o * lax.rsqrt(jnp.mean(o * o, axis=-1, keepdims=True) + 1e-5) * gain
            o_ref[rows, sl] = o.astype(o_ref.dtype)


def _diff_attn(q, k, v, lam_stack, subln_stack, layer, lambda_init, batch, seq, tq=512, heads=8):
    t = q.shape[0]
    nq = seq // tq
    wide = heads * LANE
    vmem = _vmem_estimate(
        [((tq, wide), BF16), ((seq, wide), BF16), ((seq, 2 * wide), BF16), ((tq, wide), BF16)],
        ([((tq, seq), F32)] * 2 + [((tq, seq), BF16)] * 2) * 2 * heads)
    return pl.pallas_call(
        functools.partial(_diff_attn_kernel, lambda_init=lambda_init, heads=heads),
        grid=(batch, DIFF_HEADS // heads, nq),
        in_specs=[
            pl.BlockSpec((tq, wide), lambda b, h, i: (b * nq + i, h)),
            pl.BlockSpec((seq, wide), lambda b, h, i: (b, h)),
            pl.BlockSpec((seq, 2 * wide), lambda b, h, i: (b, h)),
            _layer_block(lam_stack, layer),
            _layer_block(subln_stack, layer),
        ],
        out_specs=pl.BlockSpec((tq, wide), lambda b, h, i: (b * nq + i, h)),
        out_shape=jax.ShapeDtypeStruct((t, DIFF_OUT), BF16),
        compiler_params=_compiler_params(("parallel", "parallel", "arbitrary"), vmem),
        name="diff_attn",
    )(q, k, v, lam_stack, subln_stack)


def _mem_attn_kernel(q_ref, kv_ref, o_ref):
    for hd in range(MEM_HEADS):
        sl = slice(hd * MEM_HD, (hd + 1) * MEM_HD)
        vsl = slice(MEM_OUT + hd * MEM_HD, MEM_OUT + (hd + 1) * MEM_HD)
        s = _dot_nt(q_ref[:, sl], kv_ref[:, sl])
        p = jnp.exp(s - jnp.max(s, axis=-1, keepdims=True))
        inv_l = 1.0 / jnp.sum(p, axis=-1, keepdims=True)
        o_ref[:, sl] = (_dot(p.astype(BF16), kv_ref[:, vsl]) * inv_l).astype(o_ref.dtype)


def _mem_attn(q, kv, batch, seq, mem_len, tq=1024):
    t = q.shape[0]
    nq = seq // tq
    vmem = _vmem_estimate(
        [((tq, MEM_OUT), BF16), ((mem_len, 2 * MEM_OUT), BF16), ((tq, MEM_OUT), BF16)],
        [((tq, mem_len), F32)] * 3 + [((tq, MEM_OUT), F32)])
    return pl.pallas_call(
        _mem_attn_kernel,
        grid=(batch, nq),
        in_specs=[
            pl.BlockSpec((tq, MEM_OUT), lambda b, i: (b * nq + i, 0)),
            pl.BlockSpec((mem_len, 2 * MEM_OUT), lambda b, i: (b, 0)),
        ],
        out_specs=pl.BlockSpec((tq, MEM_OUT), lambda b, i: (b * nq + i, 0)),
        out_shape=jax.ShapeDtypeStruct((t, MEM_OUT), BF16),
        compiler_params=_compiler_params(("parallel", "arbitrary"), vmem),
        name="mem_attn",
    )(q, kv)


MERGE_CHUNKS = 2


def _merge_kernel(*refs, pieces):
    n_in = sum(pieces)
    x_ref, xb_ref = refs[:2]
    branch_refs = refs[2:2 + n_in]
    wg_ref, bg_ref, wb_ref, wo_ref, g_ref, b_ref, y_ref, yb_ref = refs[2 + n_in:]
    chunk = x_ref.shape[0] // MERGE_CHUNKS
    for c in range(MERGE_CHUNKS):
        rows = slice(c * chunk, (c + 1) * chunk)
        xb = xb_ref[rows, :]
        merged = None
        row = 0
        piece = 0
        for i, count in enumerate(pieces):
            sl = slice(i * D_MODEL, (i + 1) * D_MODEL)
            gate = jax.nn.sigmoid(_dot(xb, wg_ref[:, sl]) + bg_ref[i:i + 1, :])
            proj = None
            for o_ref in branch_refs[piece:piece + count]:
                width = o_ref.shape[1]
                part = _dot(o_ref[rows, :], wb_ref[row:row + width, :])
                proj = part if proj is None else proj + part
                row += width
            piece += count
            merged = gate * proj if merged is None else merged + gate * proj
        z = DEEPNORM_ALPHA * x_ref[rows, :] + _dot(merged.astype(BF16), wo_ref[...])
        y = _layernorm(z, g_ref[...], b_ref[...])
        y_ref[rows, :] = y
        yb_ref[rows, :] = y.astype(BF16)


def _merge(x, xb, branches, wg_stack, bg_stack, wb_stack, wo_stack, g_stack, b_stack, layer, tm=512):
    t = x.shape[0]
    rows = lambda n: pl.BlockSpec((tm, n), lambda i: (i, 0))
    flat = [a for group in branches for a in group]
    vmem = _vmem_estimate(
        [((tm, D_MODEL), F32), ((tm, D_MODEL), BF16)] + [((tm, a.shape[1]), BF16) for a in flat]
        + [(wg_stack.shape[1:], BF16), (wb_stack.shape[1:], BF16), (wo_stack.shape[1:], BF16),
           ((tm, D_MODEL), F32), ((tm, D_MODEL), BF16)],
        [((tm, D_MODEL), F32)] * 5)
    return pl.pallas_call(
        functools.partial(_merge_kernel, pieces=tuple(len(group) for group in branches)),
        grid=(t // tm,),
        in_specs=[rows(D_MODEL), rows(D_MODEL)] + [rows(a.shape[1]) for a in flat]
        + [_layer_block(wg_stack, layer), _layer_block(bg_stack, layer),
           _layer_block(wb_stack, layer), _layer_block(wo_stack, layer),
           _layer_block(g_stack, layer), _layer_block(b_stack, layer)],
        out_specs=[rows(D_MODEL), rows(D_MODEL)],
        out_shape=[jax.ShapeDtypeStruct((t, D_MODEL), F32), jax.ShapeDtypeStruct((t, D_MODEL), BF16)],
        compiler_params=_compiler_params(("parallel",), vmem),
        name="merge",
    )(x, xb, *flat, wg_stack, bg_stack, wb_stack, wo_stack, g_stack, b_stack)


MLP_EPILOGUE_CHUNKS = 4


def _mlp_kernel(x_ref, xb_ref, w1_ref, w2_ref, g_ref, b_ref, y_ref, yb_ref, acc_ref):
    kf = pl.program_id(1)

    def hidden():
        h = jnp.maximum(_dot(xb_ref[...], w1_ref[...]), 0.0)
        return (h * h).astype(BF16)

    last = pl.num_programs(1) - 1

    @pl.when(kf == 0)
    def _():
        acc_ref[...] = _dot(hidden(), w2_ref[...])

    @pl.when((kf > 0) & (kf < last))
    def _():
        acc_ref[...] += _dot(hidden(), w2_ref[...])

    @pl.when(kf == last)
    def _():
        hh = hidden()
        chunk = x_ref.shape[0] // MLP_EPILOGUE_CHUNKS
        for c in range(MLP_EPILOGUE_CHUNKS):
            rows = slice(c * chunk, (c + 1) * chunk)
            z = DEEPNORM_ALPHA * x_ref[rows, :] + (acc_ref[rows, :] + _dot(hh[rows, :], w2_ref[...]))
            y = _layernorm(z, g_ref[...], b_ref[...])
            y_ref[rows, :] = y
            yb_ref[rows, :] = y.astype(BF16)


def _mlp(x, xb, w1_stack, w2_stack, g_stack, b_stack, layer, tm=1024, tf=1024):
    t = x.shape[0]
    vmem = _vmem_estimate(
        [((tm, D_MODEL), F32), ((tm, D_MODEL), BF16), ((D_MODEL, tf), BF16), ((tf, D_MODEL), BF16),
         ((tm, D_MODEL), F32), ((tm, D_MODEL), BF16)],
        [((tm, D_MODEL), F32), ((tm, tf), F32), ((tm, tf), F32), ((tm, tf), BF16), ((tm, D_MODEL), F32)])
    return pl.pallas_call(
        _mlp_kernel,
        grid=(t // tm, D_FF // tf),
        in_specs=[
            pl.BlockSpec((tm, D_MODEL), lambda i, k: (i, 0)),
            pl.BlockSpec((tm, D_MODEL), lambda i, k: (i, 0)),
            pl.BlockSpec((None, D_MODEL, tf), lambda i, k: (layer, 0, k)),
            pl.BlockSpec((None, tf, D_MODEL), lambda i, k: (layer, k, 0)),
            _layer_block(g_stack, layer),
            _layer_block(b_stack, layer),
        ],
        out_specs=[pl.BlockSpec((tm, D_MODEL), lambda i, k: (i, 0))] * 2,
        out_shape=[jax.ShapeDtypeStruct((t, D_MODEL), F32), jax.ShapeDtypeStruct((t, D_MODEL), BF16)],
        scratch_shapes=[pltpu.VMEM((tm, D_MODEL), F32)],
        compiler_params=_compiler_params(("parallel", "arbitrary"), vmem),
        name="mlp",
    )(x, xb, w1_stack, w2_stack, g_stack, b_stack)


def _lane_patterns():
    lane = jnp.arange(LANE)
    inv_m = ROPE_THETA ** (-jnp.arange(0, MLA_ROPE, 2, dtype=F32) / MLA_ROPE)
    rel = lane - MLA_ROPE_LO
    in_rope = (rel >= 0) & (rel < MLA_ROPE)
    inv_mla = jnp.where(in_rope, inv_m[jnp.clip(rel, 0, MLA_ROPE - 1) % MLA_ROPE_HALF], 0.0)
    sgn_mla = jnp.where(in_rope, jnp.where(rel < MLA_ROPE_HALF, -1.0, 1.0), 0.0)
    inv_d = ROPE_THETA ** (-jnp.arange(0, DIFF_ROT, 2, dtype=F32) / DIFF_ROT)
    rel = lane % DIFF_D
    in_rope = rel < DIFF_ROT
    inv_diff = jnp.where(in_rope, inv_d[rel % DIFF_ROPE_HALF], 0.0)
    sgn_diff = jnp.where(in_rope, jnp.where(rel < DIFF_ROPE_HALF, -1.0, 1.0), 0.0)
    row = lambda a: a.astype(F32).reshape(1, LANE)
    return row(inv_mla), row(sgn_mla), row(inv_diff), row(sgn_diff)


def _weight_stacks(w_in, mla_w_uq, mla_w_ukv):
    depth, d, _ = w_in.shape
    lat = MLA_Q_LORA + MLA_KV_LORA
    branch_lo = lat + MLA_ROPE
    gate_lo = branch_lo + 2 * DIFF_QK + DIFF_OUT + MEM_OUT
    w_a = jnp.concatenate(
        [w_in[:, :, :lat], jnp.zeros((depth, d, MLA_ROPE_LO), F32), w_in[:, :, lat:branch_lo],
         jnp.zeros((depth, d, LANE - MLA_ROPE_LO - MLA_ROPE), F32)], axis=2).astype(BF16)
    w_branch_in = w_in[:, :, branch_lo:gate_lo].astype(BF16)
    w_gate = w_in[:, :, gate_lo:].astype(BF16)

    qd = MLA_NOPE + MLA_ROPE
    w_uq = jnp.pad(mla_w_uq.reshape(depth, MLA_Q_LORA, MLA_HEADS, qd), ((0, 0), (0, 0), (0, 0), (0, LANE - qd)))
    lane = jnp.arange(LANE)
    rel = lane - MLA_ROPE_LO
    in_rope = (rel >= 0) & (rel < MLA_ROPE)
    partner = jnp.where(rel < MLA_ROPE_HALF, lane + MLA_ROPE_HALF, lane - MLA_ROPE_HALF)
    w_uq_partner = jnp.where(in_rope, jnp.take(w_uq, jnp.clip(partner, 0, LANE - 1), axis=3), 0.0)
    flat = lambda a: a.reshape(depth, a.shape[1], -1).astype(BF16)
    ukv = mla_w_ukv.reshape(depth, MLA_KV_LORA, MLA_HEADS, MLA_NOPE + MLA_V)
    w_uk = jnp.pad(ukv[..., :MLA_NOPE], ((0, 0), (0, 0), (0, 0), (0, LANE - MLA_NOPE)))
    return w_a, w_branch_in, w_gate, flat(w_uq), flat(w_uq_partner), flat(w_uk), flat(ukv[..., MLA_NOPE:])


def kernel(x, mem, positions, w_in, b_gate, mla_q_norm, mla_kv_norm, mla_w_uq, mla_w_ukv, diff_lambda, diff_subln,
           mem_w_kv, w_branch, w_out, ln1_g, ln1_b, mlp_w1, mlp_w2, ln2_g, ln2_b):
    batch, seq, d = x.shape
    mem_len = mem.shape[1]
    t = batch * seq
    depth = w_in.shape[0]

    inv_mla, sgn_mla, inv_diff, sgn_diff = _lane_patterns()
    pos = positions.reshape(t, 1)
    cos_m, sin_m = _rope_tables(pos, inv_mla, sgn_mla)
    cos_d, sin_d = _rope_tables(pos, inv_diff, sgn_diff)

    xf = x.reshape(t, d)
    xb = xf.astype(BF16)
    memb = mem.reshape(batch * mem_len, d).astype(BF16)

    w_a, w_branch_in, w_gate, w_uq, w_uq_partner, w_uk, w_uv = _weight_stacks(w_in, mla_w_uq, mla_w_ukv)
    mem_w_kv_b, w_branch_b, w_out_b, w1_b, w2_b = (
        a.astype(BF16) for a in (mem_w_kv, w_branch, w_out, mlp_w1, mlp_w2))
    stack_rows = lambda a: a.reshape(depth, 1, -1)
    q_norm_s, kv_norm_s, subln_s, ln1_g_s, ln1_b_s, ln2_g_s, ln2_b_s = map(
        stack_rows, (mla_q_norm, mla_kv_norm, diff_subln, ln1_g, ln1_b, ln2_g, ln2_b))

    for l in range(depth):
        lambda_init = 0.8 - 0.6 * math.exp(-0.3 * l)
        q_m, k_m, v_m = _mla_proj(xb, w_a, q_norm_s, kv_norm_s, w_uq, w_uq_partner, w_uk, w_uv, cos_m, sin_m, l)
        o_mla = _mla_attn(q_m, k_m, v_m, batch, seq)

        q_d, k_d, v_d, q_e = _branch_proj(xb, w_branch_in, cos_d, sin_d, l)
        o_diff = _diff_attn(q_d, k_d, v_d, diff_lambda, subln_s, l, lambda_init, batch, seq)

        kv_e = _mem_kv_proj(memb, mem_w_kv_b, l, tm=mem_len)
        o_mem = _mem_attn(q_e, kv_e, batch, seq, mem_len)

        xf, xb = _merge(xf, xb, ((o_mla,), (o_diff,), (o_mem,)), w_gate, b_gate, w_branch_b, w_out_b,
                        ln1_g_s, ln1_b_s, l)
        xf, xb = _mlp(xf, xb, w1_b, w2_b, ln2_g_s, ln2_b_s, l)
    return xf.reshape(batch, seq, d)
```

```python
import functools
import math

import jax
import jax.numpy as jnp
from jax import lax
from jax.experimental import pallas as pl
from jax.experimental.pallas import tpu as pltpu

D_MODEL = 1024
DEPTH = 4
ROPE_THETA = 500000.0
MLA_HEADS = 8
MLA_NOPE = 64
MLA_ROPE = 32
MLA_V = 64
MLA_Q_LORA = 384
MLA_KV_LORA = 256
DIFF_HEADS = 8
DIFF_D = 64
DIFF_ROT = DIFF_D // 4
MEM_HEADS = 4
MEM_HD = 128
N_BRANCH = 3
D_FF = 4 * D_MODEL
DEEPNORM_ALPHA = (2 * DEPTH) ** 0.25
LN_EPS = 1e-5
MLA_OUT = MLA_HEADS * MLA_V
DIFF_QK = DIFF_HEADS * 2 * DIFF_D
DIFF_OUT = DIFF_HEADS * 2 * DIFF_D
MEM_OUT = MEM_HEADS * MEM_HD

LANE = 128
V7X_VMEM_BYTES = 64 * 1024 * 1024

MLA_ROPE_LO = MLA_NOPE
MLA_ROPE_HALF = MLA_ROPE // 2
DIFF_ROPE_HALF = DIFF_ROT // 2

BF16 = jnp.bfloat16
F32 = jnp.float32
LOG2E = math.log2(math.e)


def _dot(a, b):
    return jnp.dot(a, b, preferred_element_type=F32)


def _dot_nt(a, b):
    return lax.dot_general(a, b, (((1,), (1,)), ((), ())), preferred_element_type=F32)


def _compiler_params(semantics, vmem_bytes):
    limit = min(int(vmem_bytes), V7X_VMEM_BYTES - 8 * 1024 * 1024)
    return pltpu.CompilerParams(dimension_semantics=semantics, vmem_limit_bytes=limit)


def _nbytes(shape, dtype):
    return math.prod(shape) * jnp.dtype(dtype).itemsize


def _vmem_estimate(blocks, temps):
    return 2 * sum(_nbytes(s, d) for s, d in blocks) + sum(_nbytes(s, d) for s, d in temps) + (4 << 20)


def _rope_rotate(y, cos, sin_signed, x1_mask, half):
    partner = jnp.where(x1_mask, pltpu.roll(y, LANE - half, 1), pltpu.roll(y, half, 1))
    return y * cos + partner * sin_signed


def _x1_mask_mla(rows):
    lane = lax.broadcasted_iota(jnp.int32, (rows, LANE), 1)
    return (lane >= MLA_ROPE_LO) & (lane < MLA_ROPE_LO + MLA_ROPE_HALF)


def _x1_mask_diff(rows):
    lane = lax.broadcasted_iota(jnp.int32, (rows, LANE), 1) % DIFF_D
    return lane < DIFF_ROPE_HALF


def _layernorm(z, g, b):
    mu = jnp.mean(z, axis=-1, keepdims=True)
    zc = z - mu
    var = jnp.mean(zc * zc, axis=-1, keepdims=True)
    return zc * lax.rsqrt(var + LN_EPS) * g + b


def _rope_table_kernel(pos_ref, inv_ref, sgn_ref, cos_ref, sin_ref):
    ang = pos_ref[...].astype(F32) * inv_ref[...]
    cos_ref[...] = jnp.cos(ang)
    sin_ref[...] = jnp.sin(ang) * sgn_ref[...]


def _rope_tables(pos, inv_lane, sgn_lane, tm=2048):
    t = pos.shape[0]
    return pl.pallas_call(
        _rope_table_kernel,
        grid=(t // tm,),
        in_specs=[
            pl.BlockSpec((tm, 1), lambda i: (i, 0)),
            pl.BlockSpec((1, LANE), lambda i: (0, 0)),
            pl.BlockSpec((1, LANE), lambda i: (0, 0)),
        ],
        out_specs=[pl.BlockSpec((tm, LANE), lambda i: (i, 0))] * 2,
        out_shape=[jax.ShapeDtypeStruct((t, LANE), F32)] * 2,
        compiler_params=_compiler_params(("parallel",), 32 << 20),
        name="rope_tables",
    )(pos, inv_lane, sgn_lane)


def _mla_proj_kernel(x_ref, wa_ref, gq_ref, gkv_ref, wuq_ref, wuqp_ref, wuk_ref, wuv_ref, cos_ref, sin_ref,
                     q_ref, k_ref, v_ref, *, scale):
    tm = x_ref.shape[0]
    h = _dot(x_ref[...], wa_ref[...])
    cq = h[:, :MLA_Q_LORA]
    ckv = h[:, MLA_Q_LORA:MLA_Q_LORA + MLA_KV_LORA]
    kpe = h[:, MLA_Q_LORA + MLA_KV_LORA:]

    def rms(v, g):
        return (v * lax.rsqrt(jnp.mean(v * v, axis=-1, keepdims=True) + 1e-6) * g).astype(BF16)

    cqn = rms(cq, gq_ref[...])
    ckvn = rms(ckv, gkv_ref[...])
    cos = cos_ref[...]
    sin = sin_ref[...]
    kpe_rot = _rope_rotate(kpe, cos, sin, _x1_mask_mla(tm), MLA_ROPE_HALF)

    q = _dot(cqn, wuq_ref[...])
    q_partner = _dot(cqn, wuqp_ref[...])
    kn = _dot(ckvn, wuk_ref[...])
    cos_q = cos * scale
    sin_q = sin * scale
    for hd in range(MLA_HEADS):
        sl = slice(hd * LANE, (hd + 1) * LANE)
        q_ref[:, sl] = (q[:, sl] * cos_q + q_partner[:, sl] * sin_q).astype(BF16)
        k_ref[:, sl] = (kn[:, sl] + kpe_rot).astype(BF16)
    v = _dot(ckvn, wuv_ref[...]).astype(BF16)
    ones = jnp.ones((tm, LANE), BF16)
    for pair in range(MLA_HEADS // 2):
        v_ref[:, 2 * pair * LANE:(2 * pair + 1) * LANE] = v[:, pair * LANE:(pair + 1) * LANE]
        v_ref[:, (2 * pair + 1) * LANE:(2 * pair + 2) * LANE] = ones


def _layer_block(stack, layer):
    return pl.BlockSpec((None,) + stack.shape[1:], lambda *_: (layer, 0, 0))


def _mla_proj(xb, wa, gq, gkv, wuq, wuqp, wuk, wuv, cos, sin, layer, tm=512):
    t = xb.shape[0]
    na = wa.shape[2]
    nq = MLA_HEADS * LANE
    nv = MLA_HEADS * LANE
    rows = lambda n: pl.BlockSpec((tm, n), lambda i: (i, 0))
    stacks = (wa, gq, gkv, wuq, wuqp, wuk, wuv)
    vmem = _vmem_estimate(
        [((tm, D_MODEL), BF16)] + [(a.shape[1:], BF16) for a in (wa, wuq, wuqp, wuk, wuv)]
        + [((tm, LANE), F32), ((tm, LANE), F32), ((tm, nq), BF16), ((tm, nq), BF16), ((tm, nv), BF16)],
        [((tm, na), F32)] + [((tm, nq), F32)] * 4)
    return pl.pallas_call(
        functools.partial(_mla_proj_kernel, scale=LOG2E * (MLA_NOPE + MLA_ROPE) ** -0.5),
        grid=(t // tm,),
        in_specs=[rows(D_MODEL)] + [_layer_block(a, layer) for a in stacks] + [rows(LANE), rows(LANE)],
        out_specs=[rows(nq), rows(nq), rows(nv)],
        out_shape=[jax.ShapeDtypeStruct((t, nq), BF16), jax.ShapeDtypeStruct((t, nq), BF16),
                   jax.ShapeDtypeStruct((t, nv), BF16)],
        compiler_params=_compiler_params(("parallel",), vmem),
        name="mla_proj",
    )(xb, *stacks, cos, sin)


def _branch_proj_kernel(x_ref, w_ref, cos_ref, sin_ref, qd_ref, kd_ref, vd_ref, qe_ref, *, q_scale, mem_scale):
    tm = x_ref.shape[0]
    x = x_ref[...]
    cos = cos_ref[...]
    sin = sin_ref[...]
    x1 = _x1_mask_diff(tm)

    def rotated(y, scale, o_ref):
        for hd in range(DIFF_HEADS):
            sl = slice(hd * LANE, (hd + 1) * LANE)
            yh = _rope_rotate(y[:, sl], cos, sin, x1, DIFF_ROPE_HALF)
            o_ref[:, sl] = (yh if scale == 1.0 else yh * scale).astype(o_ref.dtype)

    rotated(_dot(x, w_ref[:, :DIFF_QK]), q_scale, qd_ref)
    rotated(_dot(x, w_ref[:, DIFF_QK:2 * DIFF_QK]), 1.0, kd_ref)
    v = _dot(x, w_ref[:, 2 * DIFF_QK:2 * DIFF_QK + DIFF_OUT]).astype(vd_ref.dtype)
    ones = jnp.ones((tm, LANE), vd_ref.dtype)
    for hd in range(DIFF_HEADS):
        vd_ref[:, 2 * hd * LANE:(2 * hd + 1) * LANE] = v[:, hd * LANE:(hd + 1) * LANE]
        vd_ref[:, (2 * hd + 1) * LANE:(2 * hd + 2) * LANE] = ones
    qe_ref[...] = (_dot(x, w_ref[:, 2 * DIFF_QK + DIFF_OUT:]) * mem_scale).astype(qe_ref.dtype)


def _branch_proj(xb, w_stack, cos, sin, layer, tm=512):
    t, kdim = xb.shape
    n = w_stack.shape[2]
    rows = lambda width: pl.BlockSpec((tm, width), lambda i: (i, 0))
    widths = (DIFF_QK, DIFF_QK, 2 * DIFF_OUT, MEM_OUT)
    vmem = _vmem_estimate(
        [((tm, kdim), BF16), ((kdim, n), BF16), ((tm, LANE), F32), ((tm, LANE), F32)]
        + [((tm, width), BF16) for width in widths],
        [((tm, DIFF_QK), F32)] * 6)
    return pl.pallas_call(
        functools.partial(_branch_proj_kernel, q_scale=LOG2E * DIFF_D ** -0.5, mem_scale=MEM_HD ** -0.5),
        grid=(t // tm,),
        in_specs=[rows(kdim), _layer_block(w_stack, layer), rows(LANE), rows(LANE)],
        out_specs=[rows(width) for width in widths],
        out_shape=[jax.ShapeDtypeStruct((t, width), BF16) for width in widths],
        compiler_params=_compiler_params(("parallel",), vmem),
        name="branch_proj",
    )(xb, w_stack, cos, sin)


def _mem_kv_kernel(x_ref, w_ref, o_ref):
    o_ref[...] = _dot(x_ref[...], w_ref[...]).astype(o_ref.dtype)


def _mem_kv_proj(memb, w_stack, layer, tm):
    t, kdim = memb.shape
    n = w_stack.shape[2]
    vmem = _vmem_estimate([((tm, kdim), BF16), ((kdim, n), BF16), ((tm, n), BF16)], [((tm, n), F32)] * 2)
    return pl.pallas_call(
        _mem_kv_kernel,
        grid=(t // tm,),
        in_specs=[pl.BlockSpec((tm, kdim), lambda i: (i, 0)),
                  _layer_block(w_stack, layer)],
        out_specs=pl.BlockSpec((tm, n), lambda i: (i, 0)),
        out_shape=jax.ShapeDtypeStruct((t, n), BF16),
        compiler_params=_compiler_params(("parallel",), vmem),
        name="mem_kv_proj",
    )(memb, w_stack)


def _softmax_pv(q, k, v_ext):
    return _normalized_pv(_softmax_numerators(q, k), v_ext)


def _softmax_numerators(q, k):
    return _numerators_from_scores(_dot_nt(q, k))


def _numerators_from_scores(s):
    m = jnp.max(s, axis=-1, keepdims=True)
    return jnp.exp2((s - m).astype(BF16))


def _normalized_pv(p, v_ext):
    pv = _dot(p, v_ext)
    return pv[:, :LANE] / pv[:, LANE:]


ATTN_ROWS = 256


def _mla_attn_kernel(q_ref, k_ref, v_ref, o_ref, *, pairs):
    first = lax.broadcasted_iota(jnp.int32, (ATTN_ROWS, LANE), 1) < MLA_V
    for sub in range(q_ref.shape[0] // ATTN_ROWS):
        rows = slice(sub * ATTN_ROWS, (sub + 1) * ATTN_ROWS)
        for pr in range(pairs):
            v_ext = v_ref[:, 2 * pr * LANE:(2 * pr + 2) * LANE]
            outs = []
            for j in range(2):
                sl = slice((2 * pr + j) * LANE, (2 * pr + j + 1) * LANE)
                outs.append(_softmax_pv(q_ref[rows, sl], k_ref[:, sl], v_ext))
            o_ref[rows, pr * LANE:(pr + 1) * LANE] = jnp.where(first, outs[0], outs[1]).astype(o_ref.dtype)


def _mla_attn(q, k, v, batch, seq, tq=1024, pairs=4):
    t = q.shape[0]
    nq = seq // tq
    wide = 2 * pairs * LANE
    grid = (batch, MLA_HEADS // (2 * pairs), nq)
    vmem = _vmem_estimate(
        [((tq, wide), BF16), ((seq, wide), BF16), ((seq, wide), BF16), ((tq, pairs * LANE), BF16)],
        ([((tq, seq), F32)] * 2 + [((tq, seq), BF16)] * 2) * 2 * pairs)
    return pl.pallas_call(
        functools.partial(_mla_attn_kernel, pairs=pairs),
        grid=grid,
        in_specs=[
            pl.BlockSpec((tq, wide), lambda b, g, i: (b * nq + i, g)),
            pl.BlockSpec((seq, wide), lambda b, g, i: (b, g)),
            pl.BlockSpec((seq, wide), lambda b, g, i: (b, g)),
        ],
        out_specs=pl.BlockSpec((tq, pairs * LANE), lambda b, g, i: (b * nq + i, g)),
        out_shape=jax.ShapeDtypeStruct((t, MLA_OUT), BF16),
        compiler_params=_compiler_params(("parallel", "parallel", "arbitrary"), vmem),
        name="mla_attn",
    )(q, k, v)


def _diff_attn_kernel(q_ref, k_ref, v_ref, lam_ref, g_ref, o_ref, *, lambda_init, heads):
    tq = q_ref.shape[0]
    lam = lam_ref[...]
    lam_a = jnp.sum(lam[0:1] * lam[1:2], axis=-1, keepdims=True)
    lam_b = jnp.sum(lam[2:3] * lam[3:4], axis=-1, keepdims=True)
    lambda_full = jnp.exp(lam_a) - jnp.exp(lam_b) + lambda_init
    map0 = lax.broadcasted_iota(jnp.int32, (ATTN_ROWS, LANE), 1) < DIFF_D
    gain = g_ref[...] * (1.0 - lambda_init)
    for sub in range(tq // ATTN_ROWS):
        rows = slice(sub * ATTN_ROWS, (sub + 1) * ATTN_ROWS)
        for hd in range(heads):
            sl = slice(hd * LANE, (hd + 1) * LANE)
            q = q_ref[rows, sl]
            k = k_ref[:, sl]
            v_ext = v_ref[:, 2 * hd * LANE:(2 * hd + 2) * LANE]
            zero = jnp.zeros_like(q)
            o = _softmax_pv(jnp.where(map0, q, zero), k, v_ext) - lambda_full * _softmax_pv(
                jnp.where(map0, zero, q), k, v_ext)
            o = o * lax.rsqrt(jnp.mean(o * o, axis=-1, keepdims=True) + 1e-5) * gain
            o_ref[rows, sl] = o.astype(o_ref.dtype)


def _diff_attn(q, k, v, lam_stack, subln_stack, layer, lambda_init, batch, seq, tq=512, heads=8):
    t = q.shape[0]
    nq = seq // tq
    wide = heads * LANE
    vmem = _vmem_estimate(
        [((tq, wide), BF16), ((seq, wide), BF16), ((seq, 2 * wide), BF16), ((tq, wide), BF16)],
        ([((tq, seq), F32)] * 2 + [((tq, seq), BF16)] * 2) * 2 * heads)
    return pl.pallas_call(
        functools.partial(_diff_attn_kernel, lambda_init=lambda_init, heads=heads),
        grid=(batch, DIFF_HEADS // heads, nq),
        in_specs=[
            pl.BlockSpec((tq, wide), lambda b, h, i: (b * nq + i, h)),
            pl.BlockSpec((seq, wide), lambda b, h, i: (b, h)),
            pl.BlockSpec((seq, 2 * wide), lambda b, h, i: (b, h)),
            _layer_block(lam_stack, layer),
            _layer_block(subln_stack, layer),
        ],
        out_specs=pl.BlockSpec((tq, wide), lambda b, h, i: (b * nq + i, h)),
        out_shape=jax.ShapeDtypeStruct((t, DIFF_OUT), BF16),
        compiler_params=_compiler_params(("parallel", "parallel", "arbitrary"), vmem),
        name="diff_attn",
    )(q, k, v, lam_stack, subln_stack)


def _mem_attn_kernel(q_ref, kv_ref, o_ref):
    for hd in range(MEM_HEADS):
        sl = slice(hd * MEM_HD, (hd + 1) * MEM_HD)
        vsl = slice(MEM_OUT + hd * MEM_HD, MEM_OUT + (hd + 1) * MEM_HD)
        s = _dot_nt(q_ref[:, sl], kv_ref[:, sl])
        p = jnp.exp(s - jnp.max(s, axis=-1, keepdims=True))
        inv_l = 1.0 / jnp.sum(p, axis=-1, keepdims=True)
        o_ref[:, sl] = (_dot(p.astype(BF16), kv_ref[:, vsl]) * inv_l).astype(o_ref.dtype)


def _mem_attn(q, kv, batch, seq, mem_len, tq=1024):
    t = q.shape[0]
    nq = seq // tq
    vmem = _vmem_estimate(
        [((tq, MEM_OUT), BF16), ((mem_len, 2 * MEM_OUT), BF16), ((tq, MEM_OUT), BF16)],
        [((tq, mem_len), F32)] * 3 + [((tq, MEM_OUT), F32)])
    return pl.pallas_call(
        _mem_attn_kernel,
        grid=(batch, nq),
        in_specs=[
            pl.BlockSpec((tq, MEM_OUT), lambda b, i: (b * nq + i, 0)),
            pl.BlockSpec((mem_len, 2 * MEM_OUT), lambda b, i: (b, 0)),
        ],
        out_specs=pl.BlockSpec((tq, MEM_OUT), lambda b, i: (b * nq + i, 0)),
        out_shape=jax.ShapeDtypeStruct((t, MEM_OUT), BF16),
        compiler_params=_compiler_params(("parallel", "arbitrary"), vmem),
        name="mem_attn",
    )(q, kv)


MERGE_CHUNKS = 2


def _merge_kernel(*refs, pieces):
    n_in = sum(pieces)
    x_ref, xb_ref = refs[:2]
    branch_refs = refs[2:2 + n_in]
    wg_ref, bg_ref, wb_ref, wo_ref, g_ref, b_ref, y_ref, yb_ref = refs[2 + n_in:]
    chunk = x_ref.shape[0] // MERGE_CHUNKS
    for c in range(MERGE_CHUNKS):
        rows = slice(c * chunk, (c + 1) * chunk)
        xb = xb_ref[rows, :]
        merged = None
        row = 0
        piece = 0
        for i, count in enumerate(pieces):
            sl = slice(i * D_MODEL, (i + 1) * D_MODEL)
            gate = jax.nn.sigmoid(_dot(xb, wg_ref[:, sl]) + bg_ref[i:i + 1, :])
            proj = None
            for o_ref in branch_refs[piece:piece + count]:
                width = o_ref.shape[1]
                part = _dot(o_ref[rows, :], wb_ref[row:row + width, :])
                proj = part if proj is None else proj + part
                row += width
            piece += count
            merged = gate * proj if merged is None else merged + gate * proj
        z = DEEPNORM_ALPHA * x_ref[rows, :] + _dot(merged.astype(BF16), wo_ref[...])
        y = _layernorm(z, g_ref[...], b_ref[...])
        y_ref[rows, :] = y
        yb_ref[rows, :] = y.astype(BF16)


def _merge(x, xb, branches, wg_stack, bg_stack, wb_stack, wo_stack, g_stack, b_stack, layer, tm=512):
    t = x.shape[0]
    rows = lambda n: pl.BlockSpec((tm, n), lambda i: (i, 0))
    flat = [a for group in branches for a in group]
    vmem = _vmem_estimate(
        [((tm, D_MODEL), F32), ((tm, D_MODEL), BF16)] + [((tm, a.shape[1]), BF16) for a in flat]
        + [(wg_stack.shape[1:], BF16), (wb_stack.shape[1:], BF16), (wo_stack.shape[1:], BF16),
           ((tm, D_MODEL), F32), ((tm, D_MODEL), BF16)],
        [((tm, D_MODEL), F32)] * 5)
    return pl.pallas_call(
        functools.partial(_merge_kernel, pieces=tuple(len(group) for group in branches)),
        grid=(t // tm,),
        in_specs=[rows(D_MODEL), rows(D_MODEL)] + [rows(a.shape[1]) for a in flat]
        + [_layer_block(wg_stack, layer), _layer_block(bg_stack, layer),
           _layer_block(wb_stack, layer), _layer_block(wo_stack, layer),
           _layer_block(g_stack, layer), _layer_block(b_stack, layer)],
        out_specs=[rows(D_MODEL), rows(D_MODEL)],
        out_shape=[jax.ShapeDtypeStruct((t, D_MODEL), F32), jax.ShapeDtypeStruct((t, D_MODEL), BF16)],
        compiler_params=_compiler_params(("parallel",), vmem),
        name="merge",
    )(x, xb, *flat, wg_stack, bg_stack, wb_stack, wo_stack, g_stack, b_stack)


MLP_EPILOGUE_CHUNKS = 4


def _mlp_kernel(x_ref, xb_ref, w1_ref, w2_ref, g_ref, b_ref, y_ref, yb_ref, acc_ref):
    kf = pl.program_id(1)

    def hidden():
        h = jnp.maximum(_dot(xb_ref[...], w1_ref[...]), 0.0)
        return (h * h).astype(BF16)

    last = pl.num_programs(1) - 1

    @pl.when(kf == 0)
    def _():
        acc_ref[...] = _dot(hidden(), w2_ref[...])

    @pl.when((kf > 0) & (kf < last))
    def _():
        acc_ref[...] += _dot(hidden(), w2_ref[...])

    @pl.when(kf == last)
    def _():
        hh = hidden()
        chunk = x_ref.shape[0] // MLP_EPILOGUE_CHUNKS
        for c in range(MLP_EPILOGUE_CHUNKS):
            rows = slice(c * chunk, (c + 1) * chunk)
            z = DEEPNORM_ALPHA * x_ref[rows, :] + (acc_ref[rows, :] + _dot(hh[rows, :], w2_ref[...]))
            y = _layernorm(z, g_ref[...], b_ref[...])
            y_ref[rows, :] = y
            yb_ref[rows, :] = y.astype(BF16)


def _mlp(x, xb, w1_stack, w2_stack, g_stack, b_stack, layer, tm=1024, tf=1024):
    t = x.shape[0]
    vmem = _vmem_estimate(
        [((tm, D_MODEL), F32), ((tm, D_MODEL), BF16), ((D_MODEL, tf), BF16), ((tf, D_MODEL), BF16),
         ((tm, D_MODEL), F32), ((tm, D_MODEL), BF16)],
        [((tm, D_MODEL), F32), ((tm, tf), F32), ((tm, tf), F32), ((tm, tf), BF16), ((tm, D_MODEL), F32)])
    return pl.pallas_call(
        _mlp_kernel,
        grid=(t // tm, D_FF // tf),
        in_specs=[
            pl.BlockSpec((tm, D_MODEL), lambda i, k: (i, 0)),
            pl.BlockSpec((tm, D_MODEL), lambda i, k: (i, 0)),
            pl.BlockSpec((None, D_MODEL, tf), lambda i, k: (layer, 0, k)),
            pl.BlockSpec((None, tf, D_MODEL), lambda i, k: (layer, k, 0)),
            _layer_block(g_stack, layer),
            _layer_block(b_stack, layer),
        ],
        out_specs=[pl.BlockSpec((tm, D_MODEL), lambda i, k: (i, 0))] * 2,
        out_shape=[jax.ShapeDtypeStruct((t, D_MODEL), F32), jax.ShapeDtypeStruct((t, D_MODEL), BF16)],
        scratch_shapes=[pltpu.VMEM((tm, D_MODEL), F32)],
        compiler_params=_compiler_params(("parallel", "arbitrary"), vmem),
        name="mlp",
    )(x, xb, w1_stack, w2_stack, g_stack, b_stack)


def _lane_patterns():
    lane = jnp.arange(LANE)
    inv_m = ROPE_THETA ** (-jnp.arange(0, MLA_ROPE, 2, dtype=F32) / MLA_ROPE)
    rel = lane - MLA_ROPE_LO
    in_rope = (rel >= 0) & (rel < MLA_ROPE)
    inv_mla = jnp.where(in_rope, inv_m[jnp.clip(rel, 0, MLA_ROPE - 1) % MLA_ROPE_HALF], 0.0)
    sgn_mla = jnp.where(in_rope, jnp.where(rel < MLA_ROPE_HALF, -1.0, 1.0), 0.0)
    inv_d = ROPE_THETA ** (-jnp.arange(0, DIFF_ROT, 2, dtype=F32) / DIFF_ROT)
    rel = lane % DIFF_D
    in_rope = rel < DIFF_ROT
    inv_diff = jnp.where(in_rope, inv_d[rel % DIFF_ROPE_HALF], 0.0)
    sgn_diff = jnp.where(in_rope, jnp.where(rel < DIFF_ROPE_HALF, -1.0, 1.0), 0.0)
    row = lambda a: a.astype(F32).reshape(1, LANE)
    return row(inv_mla), row(sgn_mla), row(inv_diff), row(sgn_diff)


def _weight_stacks(w_in, mla_w_uq, mla_w_ukv):
    depth, d, _ = w_in.shape
    lat = MLA_Q_LORA + MLA_KV_LORA
    branch_lo = lat + MLA_ROPE
    gate_lo = branch_lo + 2 * DIFF_QK + DIFF_OUT + MEM_OUT
    w_a = jnp.concatenate(
        [w_in[:, :, :lat], jnp.zeros((depth, d, MLA_ROPE_LO), F32), w_in[:, :, lat:branch_lo],
         jnp.zeros((depth, d, LANE - MLA_ROPE_LO - MLA_ROPE), F32)], axis=2).astype(BF16)
    w_branch_in = w_in[:, :, branch_lo:gate_lo].astype(BF16)
    w_gate = w_in[:, :, gate_lo:].astype(BF16)

    qd = MLA_NOPE + MLA_ROPE
    w_uq = jnp.pad(mla_w_uq.reshape(depth, MLA_Q_LORA, MLA_HEADS, qd), ((0, 0), (0, 0), (0, 0), (0, LANE - qd)))
    lane = jnp.arange(LANE)
    rel = lane - MLA_ROPE_LO
    in_rope = (rel >= 0) & (rel < MLA_ROPE)
    partner = jnp.where(rel < MLA_ROPE_HALF, lane + MLA_ROPE_HALF, lane - MLA_ROPE_HALF)
    w_uq_partner = jnp.where(in_rope, jnp.take(w_uq, jnp.clip(partner, 0, LANE - 1), axis=3), 0.0)
    flat = lambda a: a.reshape(depth, a.shape[1], -1).astype(BF16)
    ukv = mla_w_ukv.reshape(depth, MLA_KV_LORA, MLA_HEADS, MLA_NOPE + MLA_V)
    w_uk = jnp.pad(ukv[..., :MLA_NOPE], ((0, 0), (0, 0), (0, 0), (0, LANE - MLA_NOPE)))
    return w_a, w_branch_in, w_gate, flat(w_uq), flat(w_uq_partner), flat(w_uk), flat(ukv[..., MLA_NOPE:])


def kernel(x, mem, positions, w_in, b_gate, mla_q_norm, mla_kv_norm, mla_w_uq, mla_w_ukv, diff_lambda, diff_subln,
           mem_w_kv, w_branch, w_out, ln1_g, ln1_b, mlp_w1, mlp_w2, ln2_g, ln2_b):
    batch, seq, d = x.shape
    mem_len = mem.shape[1]
    t = batch * seq
    depth = w_in.shape[0]

    inv_mla, sgn_mla, inv_diff, sgn_diff = _lane_patterns()
    pos = positions.reshape(t, 1)
    cos_m, sin_m = _rope_tables(pos, inv_mla, sgn_mla)
    cos_d, sin_d = _rope_tables(pos, inv_diff, sgn_diff)

    xf = x.reshape(t, d)
    xb = xf.astype(BF16)
    memb = mem.reshape(batch * mem_len, d).astype(BF16)

    w_a, w_branch_in, w_gate, w_uq, w_uq_partner, w_uk, w_uv = _weight_stacks(w_in, mla_w_uq, mla_w_ukv)
    mem_w_kv_b, w_branch_b, w_out_b, w1_b, w2_b = (
        a.astype(BF16) for a in (mem_w_kv, w_branch, w_out, mlp_w1, mlp_w2))
    stack_rows = lambda a: a.reshape(depth, 1, -1)
    q_norm_s, kv_norm_s, subln_s, ln1_g_s, ln1_b_s, ln2_g_s, ln2_b_s = map(
        stack_rows, (mla_q_norm, mla_kv_norm, diff_subln, ln1_g, ln1_b, ln2_g, ln2_b))

    for l in range(depth):
        lambda_init = 0.8 - 0.6 * math.exp(-0.3 * l)
        q_m, k_m, v_m = _mla_proj(xb, w_a, q_norm_s, kv_norm_s, w_uq, w_uq_partner, w_uk, w_uv, cos_m, sin_m, l)
        o_mla = _mla_attn(q_m, k_m, v_m, batch, seq)

        q_d, k_d, v_d, q_e = _branch_proj(xb, w_branch_in, cos_d, sin_d, l)
        o_diff = _diff_attn(q_d, k_d, v_d, diff_lambda, subln_s, l, lambda_init, batch, seq)

        kv_e = _mem_kv_proj(memb, mem_w_kv_b, l, tm=mem_len)
        o_mem = _mem_attn(q_e, kv_e, batch, seq, mem_len)

        xf, xb = _merge(xf, xb, ((o_mla,), (o_diff,), (o_mem,)), w_gate, b_gate, w_branch_b, w_out_b,
                        ln1_g_s, ln1_b_s, l)
        xf, xb = _mlp(xf, xb, w1_b, w2_b, ln2_g_s, ln2_b_s, l)
    return xf.reshape(batch, seq, d)
```

```python
import functools
import math

import jax
import jax.numpy as jnp
from jax import lax
from jax.experimental import pallas as pl
from jax.experimental.pallas import tpu as pltpu

D_MODEL = 1024
DEPTH = 4
ROPE_THETA = 500000.0
MLA_HEADS = 8
MLA_NOPE = 64
MLA_ROPE = 32
MLA_V = 64
MLA_Q_LORA = 384
MLA_KV_LORA = 256
DIFF_HEADS = 8
DIFF_D = 64
DIFF_ROT = DIFF_D // 4
MEM_HEADS = 4
MEM_HD = 128
N_BRANCH = 3
D_FF = 4 * D_MODEL
DEEPNORM_ALPHA = (2 * DEPTH) ** 0.25
LN_EPS = 1e-5
MLA_OUT = MLA_HEADS * MLA_V
DIFF_QK = DIFF_HEADS * 2 * DIFF_D
DIFF_OUT = DIFF_HEADS * 2 * DIFF_D
MEM_OUT = MEM_HEADS * MEM_HD

LANE = 128
V7X_VMEM_BYTES = 64 * 1024 * 1024

MLA_ROPE_LO = MLA_NOPE
MLA_ROPE_HALF = MLA_ROPE // 2
DIFF_ROPE_HALF = DIFF_ROT // 2

BF16 = jnp.bfloat16
F32 = jnp.float32
LOG2E = math.log2(math.e)


def _dot(a, b):
    return jnp.dot(a, b, preferred_element_type=F32)


def _dot_nt(a, b):
    return lax.dot_general(a, b, (((1,), (1,)), ((), ())), preferred_element_type=F32)


def _compiler_params(semantics, vmem_bytes):
    limit = min(int(vmem_bytes), V7X_VMEM_BYTES - 8 * 1024 * 1024)
    return pltpu.CompilerParams(dimension_semantics=semantics, vmem_limit_bytes=limit)


def _nbytes(shape, dtype):
    return math.prod(shape) * jnp.dtype(dtype).itemsize


def _vmem_estimate(blocks, temps):
    return 2 * sum(_nbytes(s, d) for s, d in blocks) + sum(_nbytes(s, d) for s, d in temps) + (4 << 20)


def _rope_rotate(y, cos, sin_signed, x1_mask, half):
    partner = jnp.where(x1_mask, pltpu.roll(y, LANE - half, 1), pltpu.roll(y, half, 1))
    return y * cos + partner * sin_signed


def _x1_mask_mla(rows):
    lane = lax.broadcasted_iota(jnp.int32, (rows, LANE), 1)
    return (lane >= MLA_ROPE_LO) & (lane < MLA_ROPE_LO + MLA_ROPE_HALF)


def _x1_mask_diff(rows):
    lane = lax.broadcasted_iota(jnp.int32, (rows, LANE), 1) % DIFF_D
    return lane < DIFF_ROPE_HALF


def _layernorm(z, g, b):
    mu = jnp.mean(z, axis=-1, keepdims=True)
    zc = z - mu
    var = jnp.mean(zc * zc, axis=-1, keepdims=True)
    return zc * lax.rsqrt(var + LN_EPS) * g + b


def _rope_table_kernel(pos_ref, inv_ref, sgn_ref, cos_ref, sin_ref):
    ang = pos_ref[...].astype(F32) * inv_ref[...]
    cos_ref[...] = jnp.cos(ang)
    sin_ref[...] = jnp.sin(ang) * sgn_ref[...]


def _rope_tables(pos, inv_lane, sgn_lane, tm=2048):
    t = pos.shape[0]
    return pl.pallas_call(
        _rope_table_kernel,
        grid=(t // tm,),
        in_specs=[
            pl.BlockSpec((tm, 1), lambda i: (i, 0)),
            pl.BlockSpec((1, LANE), lambda i: (0, 0)),
            pl.BlockSpec((1, LANE), lambda i: (0, 0)),
        ],
        out_specs=[pl.BlockSpec((tm, LANE), lambda i: (i, 0))] * 2,
        out_shape=[jax.ShapeDtypeStruct((t, LANE), F32)] * 2,
        compiler_params=_compiler_params(("parallel",), 32 << 20),
        name="rope_tables",
    )(pos, inv_lane, sgn_lane)


def _mla_proj_kernel(x_ref, wa_ref, gq_ref, gkv_ref, wuq_ref, wuqp_ref, wuk_ref, wuv_ref, cos_ref, sin_ref,
                     q_ref, k_ref, v_ref, *, scale):
    tm = x_ref.shape[0]
    h = _dot(x_ref[...], wa_ref[...])
    cq = h[:, :MLA_Q_LORA]
    ckv = h[:, MLA_Q_LORA:MLA_Q_LORA + MLA_KV_LORA]
    kpe = h[:, MLA_Q_LORA + MLA_KV_LORA:]

    def rms(v, g):
        return (v * lax.rsqrt(jnp.mean(v * v, axis=-1, keepdims=True) + 1e-6) * g).astype(BF16)

    cqn = rms(cq, gq_ref[...])
    ckvn = rms(ckv, gkv_ref[...])
    cos = cos_ref[...]
    sin = sin_ref[...]
    kpe_rot = _rope_rotate(kpe, cos, sin, _x1_mask_mla(tm), MLA_ROPE_HALF)

    q = _dot(cqn, wuq_ref[...])
    q_partner = _dot(cqn, wuqp_ref[...])
    kn = _dot(ckvn, wuk_ref[...])
    cos_q = cos * scale
    sin_q = sin * scale
    for hd in range(MLA_HEADS):
        sl = slice(hd * LANE, (hd + 1) * LANE)
        q_ref[:, sl] = (q[:, sl] * cos_q + q_partner[:, sl] * sin_q).astype(BF16)
        k_ref[:, sl] = (kn[:, sl] + kpe_rot).astype(BF16)
    v = _dot(ckvn, wuv_ref[...]).astype(BF16)
    ones = jnp.ones((tm, LANE), BF16)
    for pair in range(MLA_HEADS // 2):
        v_ref[:, 2 * pair * LANE:(2 * pair + 1) * LANE] = v[:, pair * LANE:(pair + 1) * LANE]
        v_ref[:, (2 * pair + 1) * LANE:(2 * pair + 2) * LANE] = ones


def _layer_block(stack, layer):
    return pl.BlockSpec((None,) + stack.shape[1:], lambda *_: (layer, 0, 0), pipeline_mode=pl.Buffered(1))


def _mla_proj(xb, wa, gq, gkv, wuq, wuqp, wuk, wuv, cos, sin, layer, tm=1024):
    t = xb.shape[0]
    na = wa.shape[2]
    nq = MLA_HEADS * LANE
    nv = MLA_HEADS * LANE
    rows = lambda n: pl.BlockSpec((tm, n), lambda i: (i, 0))
    stacks = (wa, gq, gkv, wuq, wuqp, wuk, wuv)
    vmem = _vmem_estimate(
        [((tm, D_MODEL), BF16)] + [(a.shape[1:], BF16) for a in (wa, wuq, wuqp, wuk, wuv)]
        + [((tm, LANE), F32), ((tm, LANE), F32), ((tm, nq), BF16), ((tm, nq), BF16), ((tm, nv), BF16)],
        [((tm, na), F32)] + [((tm, nq), F32)] * 4)
    return pl.pallas_call(
        functools.partial(_mla_proj_kernel, scale=LOG2E * (MLA_NOPE + MLA_ROPE) ** -0.5),
        grid=(t // tm,),
        in_specs=[rows(D_MODEL)] + [_layer_block(a, layer) for a in stacks] + [rows(LANE), rows(LANE)],
        out_specs=[rows(nq), rows(nq), rows(nv)],
        out_shape=[jax.ShapeDtypeStruct((t, nq), BF16), jax.ShapeDtypeStruct((t, nq), BF16),
                   jax.ShapeDtypeStruct((t, nv), BF16)],
        compiler_params=_compiler_params(("parallel",), vmem),
        name="mla_proj",
    )(xb, *stacks, cos, sin)


def _branch_proj_kernel(x_ref, w_ref, cos_ref, sin_ref, qd_ref, kd_ref, vd_ref, qe_ref, *, q_scale, mem_scale):
    tm = x_ref.shape[0]
    x = x_ref[...]
    cos = cos_ref[...]
    sin = sin_ref[...]
    x1 = _x1_mask_diff(tm)

    def rotated(y, scale, o_ref):
        for hd in range(DIFF_HEADS):
            sl = slice(hd * LANE, (hd + 1) * LANE)
            yh = _rope_rotate(y[:, sl], cos, sin, x1, DIFF_ROPE_HALF)
            o_ref[:, sl] = (yh if scale == 1.0 else yh * scale).astype(o_ref.dtype)

    rotated(_dot(x, w_ref[:, :DIFF_QK]), q_scale, qd_ref)
    rotated(_dot(x, w_ref[:, DIFF_QK:2 * DIFF_QK]), 1.0, kd_ref)
    v = _dot(x, w_ref[:, 2 * DIFF_QK:2 * DIFF_QK + DIFF_OUT]).astype(vd_ref.dtype)
    ones = jnp.ones((tm, LANE), vd_ref.dtype)
    for hd in range(DIFF_HEADS):
        vd_ref[:, 2 * hd * LANE:(2 * hd + 1) * LANE] = v[:, hd * LANE:(hd + 1) * LANE]
        vd_ref[:, (2 * hd + 1) * LANE:(2 * hd + 2) * LANE] = ones
    qe_ref[...] = (_dot(x, w_ref[:, 2 * DIFF_QK + DIFF_OUT:]) * mem_scale).astype(qe_ref.dtype)


def _branch_proj(xb, w_stack, cos, sin, layer, tm=512):
    t, kdim = xb.shape
    n = w_stack.shape[2]
    rows = lambda width: pl.BlockSpec((tm, width), lambda i: (i, 0))
    widths = (DIFF_QK, DIFF_QK, 2 * DIFF_OUT, MEM_OUT)
    vmem = _vmem_estimate(
        [((tm, kdim), BF16), ((kdim, n), BF16), ((tm, LANE), F32), ((tm, LANE), F32)]
        + [((tm, width), BF16) for width in widths],
        [((tm, DIFF_QK), F32)] * 6)
    return pl.pallas_call(
        functools.partial(_branch_proj_kernel, q_scale=LOG2E * DIFF_D ** -0.5, mem_scale=MEM_HD ** -0.5),
        grid=(t // tm,),
        in_specs=[rows(kdim), _layer_block(w_stack, layer), rows(LANE), rows(LANE)],
        out_specs=[rows(width) for width in widths],
        out_shape=[jax.ShapeDtypeStruct((t, width), BF16) for width in widths],
        compiler_params=_compiler_params(("parallel",), vmem),
        name="branch_proj",
    )(xb, w_stack, cos, sin)


def _mem_kv_kernel(x_ref, w_ref, o_ref):
    o_ref[...] = _dot(x_ref[...], w_ref[...]).astype(o_ref.dtype)


def _mem_kv_proj(memb, w_stack, layer, tm):
    t, kdim = memb.shape
    n = w_stack.shape[2]
    vmem = _vmem_estimate([((tm, kdim), BF16), ((kdim, n), BF16), ((tm, n), BF16)], [((tm, n), F32)] * 2)
    return pl.pallas_call(
        _mem_kv_kernel,
        grid=(t // tm,),
        in_specs=[pl.BlockSpec((tm, kdim), lambda i: (i, 0)),
                  _layer_block(w_stack, layer)],
        out_specs=pl.BlockSpec((tm, n), lambda i: (i, 0)),
        out_shape=jax.ShapeDtypeStruct((t, n), BF16),
        compiler_params=_compiler_params(("parallel",), vmem),
        name="mem_kv_proj",
    )(memb, w_stack)


def _softmax_pv(q, k, v_ext):
    return _normalized_pv(_softmax_numerators(q, k), v_ext)


def _softmax_numerators(q, k):
    return _numerators_from_scores(_dot_nt(q, k))


def _numerators_from_scores(s):
    m = jnp.max(s, axis=-1, keepdims=True)
    return jnp.exp2((s - m).astype(BF16))


def _normalized_pv(p, v_ext):
    pv = _dot(p, v_ext)
    return pv[:, :LANE] / pv[:, LANE:]


ATTN_ROWS = 256


def _mla_attn_kernel(q_ref, k_ref, v_ref, o_ref, *, pairs):
    first = lax.broadcasted_iota(jnp.int32, (ATTN_ROWS, LANE), 1) < MLA_V
    for sub in range(q_ref.shape[0] // ATTN_ROWS):
        rows = slice(sub * ATTN_ROWS, (sub + 1) * ATTN_ROWS)
        for pr in range(pairs):
            v_ext = v_ref[:, 2 * pr * LANE:(2 * pr + 2) * LANE]
            outs = []
            for j in range(2):
                sl = slice((2 * pr + j) * LANE, (2 * pr + j + 1) * LANE)
                outs.append(_softmax_pv(q_ref[rows, sl], k_ref[:, sl], v_ext))
            o_ref[rows, pr * LANE:(pr + 1) * LANE] = jnp.where(first, outs[0], outs[1]).astype(o_ref.dtype)


def _mla_attn(q, k, v, batch, seq, tq=1024, pairs=4):
    t = q.shape[0]
    nq = seq // tq
    wide = 2 * pairs * LANE
    grid = (batch, MLA_HEADS // (2 * pairs), nq)
    vmem = _vmem_estimate(
        [((tq, wide), BF16), ((seq, wide), BF16), ((seq, wide), BF16), ((tq, pairs * LANE), BF16)],
        ([((tq, seq), F32)] * 2 + [((tq, seq), BF16)] * 2) * 2 * pairs)
    return pl.pallas_call(
        functools.partial(_mla_attn_kernel, pairs=pairs),
        grid=grid,
        in_specs=[
            pl.BlockSpec((tq, wide), lambda b, g, i: (b * nq + i, g)),
            pl.BlockSpec((seq, wide), lambda b, g, i: (b, g)),
            pl.BlockSpec((seq, wide), lambda b, g, i: (b, g)),
        ],
        out_specs=pl.BlockSpec((tq, pairs * LANE), lambda b, g, i: (b * nq + i, g)),
        out_shape=jax.ShapeDtypeStruct((t, MLA_OUT), BF16),
        compiler_params=_compiler_params(("parallel", "parallel", "arbitrary"), vmem),
        name="mla_attn",
    )(q, k, v)


def _diff_attn_kernel(q_ref, k_ref, v_ref, lam_ref, g_ref, o_ref, *, lambda_init, heads):
    tq = q_ref.shape[0]
    lam = lam_ref[...]
    lam_a = jnp.sum(lam[0:1] * lam[1:2], axis=-1, keepdims=True)
    lam_b = jnp.sum(lam[2:3] * lam[3:4], axis=-1, keepdims=True)
    lambda_full = jnp.exp(lam_a) - jnp.exp(lam_b) + lambda_init
    map0 = lax.broadcasted_iota(jnp.int32, (ATTN_ROWS, LANE), 1) < DIFF_D
    gain = g_ref[...] * (1.0 - lambda_init)
    for sub in range(tq // ATTN_ROWS):
        rows = slice(sub * ATTN_ROWS, (sub + 1) * ATTN_ROWS)
        for hd in range(heads):
            sl = slice(hd * LANE, (hd + 1) * LANE)
            q = q_ref[rows, sl]
            k = k_ref[:, sl]
            v_ext = v_ref[:, 2 * hd * LANE:(2 * hd + 2) * LANE]
            zero = jnp.zeros_like(q)
            o = _softmax_pv(jnp.where(map0, q, zero), k, v_ext) - lambda_full * _softmax_pv(
                jnp.where(map0, zero, q), k, v_ext)
            o = o * lax.rsqrt(jnp.mean(o * o, axis=-1, keepdims=True) + 1e-5) * gain
            o_ref[rows, sl] = o.astype(o_ref.dtype)


def _diff_attn(q, k, v, lam_stack, subln_stack, layer, lambda_init, batch, seq, tq=512, heads=8):
    t = q.shape[0]
    nq = seq // tq
    wide = heads * LANE
    vmem = _vmem_estimate(
        [((tq, wide), BF16), ((seq, wide), BF16), ((seq, 2 * wide), BF16), ((tq, wide), BF16)],
        ([((tq, seq), F32)] * 2 + [((tq, seq), BF16)] * 2) * 2 * heads)
    return pl.pallas_call(
        functools.partial(_diff_attn_kernel, lambda_init=lambda_init, heads=heads),
        grid=(batch, DIFF_HEADS // heads, nq),
        in_specs=[
            pl.BlockSpec((tq, wide), lambda b, h, i: (b * nq + i, h)),
            pl.BlockSpec((seq, wide), lambda b, h, i: (b, h)),
            pl.BlockSpec((seq, 2 * wide), lambda b, h, i: (b, h)),
            _layer_block(lam_stack, layer),
            _layer_block(subln_stack, layer),
        ],
        out_specs=pl.BlockSpec((tq, wide), lambda b, h, i: (b * nq + i, h)),
        out_shape=jax.ShapeDtypeStruct((t, DIFF_OUT), BF16),
        compiler_params=_compiler_params(("parallel", "parallel", "arbitrary"), vmem),
        name="diff_attn",
    )(q, k, v, lam_stack, subln_stack)


def _mem_attn_kernel(q_ref, kv_ref, o_ref):
    for hd in range(MEM_HEADS):
        sl = slice(hd * MEM_HD, (hd + 1) * MEM_HD)
        vsl = slice(MEM_OUT + hd * MEM_HD, MEM_OUT + (hd + 1) * MEM_HD)
        s = _dot_nt(q_ref[:, sl], kv_ref[:, sl])
        p = jnp.exp(s - jnp.max(s, axis=-1, keepdims=True))
        inv_l = 1.0 / jnp.sum(p, axis=-1, keepdims=True)
        o_ref[:, sl] = (_dot(p.astype(BF16), kv_ref[:, vsl]) * inv_l).astype(o_ref.dtype)


def _mem_attn(q, kv, batch, seq, mem_len, tq=1024):
    t = q.shape[0]
    nq = seq // tq
    vmem = _vmem_estimate(
        [((tq, MEM_OUT), BF16), ((mem_len, 2 * MEM_OUT), BF16), ((tq, MEM_OUT), BF16)],
        [((tq, mem_len), F32)] * 3 + [((tq, MEM_OUT), F32)])
    return pl.pallas_call(
        _mem_attn_kernel,
        grid=(batch, nq),
        in_specs=[
            pl.BlockSpec((tq, MEM_OUT), lambda b, i: (b * nq + i, 0)),
            pl.BlockSpec((mem_len, 2 * MEM_OUT), lambda b, i: (b, 0)),
        ],
        out_specs=pl.BlockSpec((tq, MEM_OUT), lambda b, i: (b * nq + i, 0)),
        out_shape=jax.ShapeDtypeStruct((t, MEM_OUT), BF16),
        compiler_params=_compiler_params(("parallel", "arbitrary"), vmem),
        name="mem_attn",
    )(q, kv)


MERGE_ROW_CHUNK = 256


def _merge_kernel(*refs, pieces):
    n_in = sum(pieces)
    x_ref, xb_ref = refs[:2]
    branch_refs = refs[2:2 + n_in]
    wg_ref, bg_ref, wb_ref, wo_ref, g_ref, b_ref, y_ref, yb_ref = refs[2 + n_in:]
    chunk = MERGE_ROW_CHUNK
    for c in range(x_ref.shape[0] // chunk):
        rows = slice(c * chunk, (c + 1) * chunk)
        xb = xb_ref[rows, :]
        merged = None
        row = 0
        piece = 0
        for i, count in enumerate(pieces):
            sl = slice(i * D_MODEL, (i + 1) * D_MODEL)
            gate = jax.nn.sigmoid(_dot(xb, wg_ref[:, sl]) + bg_ref[i:i + 1, :])
            proj = None
            for o_ref in branch_refs[piece:piece + count]:
                width = o_ref.shape[1]
                part = _dot(o_ref[rows, :], wb_ref[row:row + width, :])
                proj = part if proj is None else proj + part
                row += width
            piece += count
            merged = gate * proj if merged is None else merged + gate * proj
        z = DEEPNORM_ALPHA * x_ref[rows, :] + _dot(merged.astype(BF16), wo_ref[...])
        y = _layernorm(z, g_ref[...], b_ref[...])
        y_ref[rows, :] = y
        yb_ref[rows, :] = y.astype(BF16)


def _merge(x, xb, branches, wg_stack, bg_stack, wb_stack, wo_stack, g_stack, b_stack, layer, tm=1024):
    t = x.shape[0]
    rows = lambda n: pl.BlockSpec((tm, n), lambda i: (i, 0))
    flat = [a for group in branches for a in group]
    vmem = _vmem_estimate(
        [((tm, D_MODEL), F32), ((tm, D_MODEL), BF16)] + [((tm, a.shape[1]), BF16) for a in flat]
        + [(wg_stack.shape[1:], BF16), (wb_stack.shape[1:], BF16), (wo_stack.shape[1:], BF16),
           ((tm, D_MODEL), F32), ((tm, D_MODEL), BF16)],
        [((tm, D_MODEL), F32)] * 5)
    return pl.pallas_call(
        functools.partial(_merge_kernel, pieces=tuple(len(group) for group in branches)),
        grid=(t // tm,),
        in_specs=[rows(D_MODEL), rows(D_MODEL)] + [rows(a.shape[1]) for a in flat]
        + [_layer_block(wg_stack, layer), _layer_block(bg_stack, layer),
           _layer_block(wb_stack, layer), _layer_block(wo_stack, layer),
           _layer_block(g_stack, layer), _layer_block(b_stack, layer)],
        out_specs=[rows(D_MODEL), rows(D_MODEL)],
        out_shape=[jax.ShapeDtypeStruct((t, D_MODEL), F32), jax.ShapeDtypeStruct((t, D_MODEL), BF16)],
        compiler_params=_compiler_params(("parallel",), vmem),
        name="merge",
    )(x, xb, *flat, wg_stack, bg_stack, wb_stack, wo_stack, g_stack, b_stack)


MLP_ROW_CHUNK = 256
MLP_HIDDEN_CHUNK = 1024


def _mlp_kernel(x_ref, xb_ref, w1_ref, w2_ref, g_ref, b_ref, y_ref, yb_ref):
    for c in range(x_ref.shape[0] // MLP_ROW_CHUNK):
        rows = slice(c * MLP_ROW_CHUNK, (c + 1) * MLP_ROW_CHUNK)
        xb = xb_ref[rows, :]
        acc = None
        for f in range(D_FF // MLP_HIDDEN_CHUNK):
            cols = slice(f * MLP_HIDDEN_CHUNK, (f + 1) * MLP_HIDDEN_CHUNK)
            h = jnp.maximum(_dot(xb, w1_ref[:, cols]), 0.0)
            part = _dot((h * h).astype(BF16), w2_ref[cols, :])
            acc = part if acc is None else acc + part
        y = _layernorm(DEEPNORM_ALPHA * x_ref[rows, :] + acc, g_ref[...], b_ref[...])
        y_ref[rows, :] = y
        yb_ref[rows, :] = y.astype(BF16)


def _mlp(x, xb, w1_stack, w2_stack, g_stack, b_stack, layer, tm=1024):
    t = x.shape[0]
    resident = lambda stack: _layer_block(stack, layer)
    vmem = (_vmem_estimate(
        [((tm, D_MODEL), F32), ((tm, D_MODEL), BF16), ((tm, D_MODEL), F32), ((tm, D_MODEL), BF16)],
        [((MLP_ROW_CHUNK, MLP_HIDDEN_CHUNK), F32)] * 4 + [((MLP_ROW_CHUNK, D_MODEL), F32)] * 4)
        + _nbytes(w1_stack.shape[1:], BF16) + _nbytes(w2_stack.shape[1:], BF16))
    return pl.pallas_call(
        _mlp_kernel,
        grid=(t // tm,),
        in_specs=[
            pl.BlockSpec((tm, D_MODEL), lambda i: (i, 0)),
            pl.BlockSpec((tm, D_MODEL), lambda i: (i, 0)),
            resident(w1_stack),
            resident(w2_stack),
            _layer_block(g_stack, layer),
            _layer_block(b_stack, layer),
        ],
        out_specs=[pl.BlockSpec((tm, D_MODEL), lambda i: (i, 0))] * 2,
        out_shape=[jax.ShapeDtypeStruct((t, D_MODEL), F32), jax.ShapeDtypeStruct((t, D_MODEL), BF16)],
        compiler_params=_compiler_params(("parallel",), vmem),
        name="mlp",
    )(x, xb, w1_stack, w2_stack, g_stack, b_stack)


def _lane_patterns():
    lane = jnp.arange(LANE)
    inv_m = ROPE_THETA ** (-jnp.arange(0, MLA_ROPE, 2, dtype=F32) / MLA_ROPE)
    rel = lane - MLA_ROPE_LO
    in_rope = (rel >= 0) & (rel < MLA_ROPE)
    inv_mla = jnp.where(in_rope, inv_m[jnp.clip(rel, 0, MLA_ROPE - 1) % MLA_ROPE_HALF], 0.0)
    sgn_mla = jnp.where(in_rope, jnp.where(rel < MLA_ROPE_HALF, -1.0, 1.0), 0.0)
    inv_d = ROPE_THETA ** (-jnp.arange(0, DIFF_ROT, 2, dtype=F32) / DIFF_ROT)
    rel = lane % DIFF_D
    in_rope = rel < DIFF_ROT
    inv_diff = jnp.where(in_rope, inv_d[rel % DIFF_ROPE_HALF], 0.0)
    sgn_diff = jnp.where(in_rope, jnp.where(rel < DIFF_ROPE_HALF, -1.0, 1.0), 0.0)
    row = lambda a: a.astype(F32).reshape(1, LANE)
    return row(inv_mla), row(sgn_mla), row(inv_diff), row(sgn_diff)


def _weight_stacks(w_in, mla_w_uq, mla_w_ukv):
    depth, d, _ = w_in.shape
    lat = MLA_Q_LORA + MLA_KV_LORA
    branch_lo = lat + MLA_ROPE
    gate_lo = branch_lo + 2 * DIFF_QK + DIFF_OUT + MEM_OUT
    w_a = jnp.concatenate(
        [w_in[:, :, :lat], jnp.zeros((depth, d, MLA_ROPE_LO), F32), w_in[:, :, lat:branch_lo],
         jnp.zeros((depth, d, LANE - MLA_ROPE_LO - MLA_ROPE), F32)], axis=2).astype(BF16)
    w_branch_in = w_in[:, :, branch_lo:gate_lo].astype(BF16)
    w_gate = w_in[:, :, gate_lo:].astype(BF16)

    qd = MLA_NOPE + MLA_ROPE
    w_uq = jnp.pad(mla_w_uq.reshape(depth, MLA_Q_LORA, MLA_HEADS, qd), ((0, 0), (0, 0), (0, 0), (0, LANE - qd)))
    lane = jnp.arange(LANE)
    rel = lane - MLA_ROPE_LO
    in_rope = (rel >= 0) & (rel < MLA_ROPE)
    partner = jnp.where(rel < MLA_ROPE_HALF, lane + MLA_ROPE_HALF, lane - MLA_ROPE_HALF)
    w_uq_partner = jnp.where(in_rope, jnp.take(w_uq, jnp.clip(partner, 0, LANE - 1), axis=3), 0.0)
    flat = lambda a: a.reshape(depth, a.shape[1], -1).astype(BF16)
    ukv = mla_w_ukv.reshape(depth, MLA_KV_LORA, MLA_HEADS, MLA_NOPE + MLA_V)
    w_uk = jnp.pad(ukv[..., :MLA_NOPE], ((0, 0), (0, 0), (0, 0), (0, LANE - MLA_NOPE)))
    return w_a, w_branch_in, w_gate, flat(w_uq), flat(w_uq_partner), flat(w_uk), flat(ukv[..., MLA_NOPE:])


def kernel(x, mem, positions, w_in, b_gate, mla_q_norm, mla_kv_norm, mla_w_uq, mla_w_ukv, diff_lambda, diff_subln,
           mem_w_kv, w_branch, w_out, ln1_g, ln1_b, mlp_w1, mlp_w2, ln2_g, ln2_b):
    batch, seq, d = x.shape
    mem_len = mem.shape[1]
    t = batch * seq
    depth = w_in.shape[0]

    inv_mla, sgn_mla, inv_diff, sgn_diff = _lane_patterns()
    pos = positions.reshape(t, 1)
    cos_m, sin_m = _rope_tables(pos, inv_mla, sgn_mla)
    cos_d, sin_d = _rope_tables(pos, inv_diff, sgn_diff)

    xf = x.reshape(t, d)
    xb = xf.astype(BF16)
    memb = mem.reshape(batch * mem_len, d).astype(BF16)

    w_a, w_branch_in, w_gate, w_uq, w_uq_partner, w_uk, w_uv = _weight_stacks(w_in, mla_w_uq, mla_w_ukv)
    mem_w_kv_b, w_branch_b, w_out_b, w1_b, w2_b = (
        a.astype(BF16) for a in (mem_w_kv, w_branch, w_out, mlp_w1, mlp_w2))
    stack_rows = lambda a: a.reshape(depth, 1, -1)
    q_norm_s, kv_norm_s, subln_s, ln1_g_s, ln1_b_s, ln2_g_s, ln2_b_s = map(
        stack_rows, (mla_q_norm, mla_kv_norm, diff_subln, ln1_g, ln1_b, ln2_g, ln2_b))

    for l in range(depth):
        lambda_init = 0.8 - 0.6 * math.exp(-0.3 * l)
        q_m, k_m, v_m = _mla_proj(xb, w_a, q_norm_s, kv_norm_s, w_uq, w_uq_partner, w_uk, w_uv, cos_m, sin_m, l)
        o_mla = _mla_attn(q_m, k_m, v_m, batch, seq)

        q_d, k_d, v_d, q_e = _branch_proj(xb, w_branch_in, cos_d, sin_d, l)
        o_diff = _diff_attn(q_d, k_d, v_d, diff_lambda, subln_s, l, lambda_init, batch, seq)

        kv_e = _mem_kv_proj(memb, mem_w_kv_b, l, tm=mem_len)
        o_mem = _mem_attn(q_e, kv_e, batch, seq, mem_len)

        xf, xb = _merge(xf, xb, ((o_mla,), (o_diff,), (o_mem,)), w_gate, b_gate, w_branch_b, w_out_b,
                        ln1_g_s, ln1_b_s, l)
        xf, xb = _mlp(xf, xb, w1_b, w2_b, ln2_g_s, ln2_b_s, l)
    return xf.reshape(batch, seq, d)
```

```python
import functools
import math

import jax
import jax.numpy as jnp
from jax import lax
from jax.experimental import pallas as pl
from jax.experimental.pallas import tpu as pltpu

D_MODEL = 1024
DEPTH = 4
ROPE_THETA = 500000.0
MLA_HEADS = 8
MLA_NOPE = 64
MLA_ROPE = 32
MLA_V = 64
MLA_Q_LORA = 384
MLA_KV_LORA = 256
DIFF_HEADS = 8
DIFF_D = 64
DIFF_ROT = DIFF_D // 4
MEM_HEADS = 4
MEM_HD = 128
N_BRANCH = 3
D_FF = 4 * D_MODEL
DEEPNORM_ALPHA = (2 * DEPTH) ** 0.25
LN_EPS = 1e-5
MLA_OUT = MLA_HEADS * MLA_V
DIFF_QK = DIFF_HEADS * 2 * DIFF_D
DIFF_OUT = DIFF_HEADS * 2 * DIFF_D
MEM_OUT = MEM_HEADS * MEM_HD

LANE = 128
V7X_VMEM_BYTES = 64 * 1024 * 1024

MLA_ROPE_LO = MLA_NOPE
MLA_ROPE_HALF = MLA_ROPE // 2
DIFF_ROPE_HALF = DIFF_ROT // 2

BF16 = jnp.bfloat16
F32 = jnp.float32
LOG2E = math.log2(math.e)


def _dot(a, b):
    return jnp.dot(a, b, preferred_element_type=F32)


def _dot_nt(a, b):
    return lax.dot_general(a, b, (((1,), (1,)), ((), ())), preferred_element_type=F32)


def _compiler_params(semantics, vmem_bytes):
    limit = min(int(vmem_bytes), V7X_VMEM_BYTES - 8 * 1024 * 1024)
    return pltpu.CompilerParams(dimension_semantics=semantics, vmem_limit_bytes=limit)


def _nbytes(shape, dtype):
    return math.prod(shape) * jnp.dtype(dtype).itemsize


def _vmem_estimate(blocks, temps):
    return 2 * sum(_nbytes(s, d) for s, d in blocks) + sum(_nbytes(s, d) for s, d in temps) + (4 << 20)


def _rope_rotate(y, cos, sin_signed, x1_mask, half):
    partner = jnp.where(x1_mask, pltpu.roll(y, LANE - half, 1), pltpu.roll(y, half, 1))
    return y * cos + partner * sin_signed


def _x1_mask_mla(rows):
    lane = lax.broadcasted_iota(jnp.int32, (rows, LANE), 1)
    return (lane >= MLA_ROPE_LO) & (lane < MLA_ROPE_LO + MLA_ROPE_HALF)


def _x1_mask_diff(rows):
    lane = lax.broadcasted_iota(jnp.int32, (rows, LANE), 1) % DIFF_D
    return lane < DIFF_ROPE_HALF


def _layernorm(z, g, b):
    mu = jnp.mean(z, axis=-1, keepdims=True)
    zc = z - mu
    var = jnp.mean(zc * zc, axis=-1, keepdims=True)
    return zc * lax.rsqrt(var + LN_EPS) * g + b


def _rope_table_kernel(pos_ref, inv_ref, sgn_ref, cos_ref, sin_ref):
    ang = pos_ref[...].astype(F32) * inv_ref[...]
    cos_ref[...] = jnp.cos(ang)
    sin_ref[...] = jnp.sin(ang) * sgn_ref[...]


def _rope_tables(pos, inv_lane, sgn_lane, tm=2048):
    t = pos.shape[0]
    return pl.pallas_call(
        _rope_table_kernel,
        grid=(t // tm,),
        in_specs=[
            pl.BlockSpec((tm, 1), lambda i: (i, 0)),
            pl.BlockSpec((1, LANE), lambda i: (0, 0)),
            pl.BlockSpec((1, LANE), lambda i: (0, 0)),
        ],
        out_specs=[pl.BlockSpec((tm, LANE), lambda i: (i, 0))] * 2,
        out_shape=[jax.ShapeDtypeStruct((t, LANE), F32)] * 2,
        compiler_params=_compiler_params(("parallel",), 32 << 20),
        name="rope_tables",
    )(pos, inv_lane, sgn_lane)


def _mla_proj_kernel(x_ref, wa_ref, gq_ref, gkv_ref, wuq_ref, wuqp_ref, wuk_ref, wuv_ref, cos_ref, sin_ref,
                     q_ref, k_ref, v_ref, *, scale):
    tm = x_ref.shape[0]
    h = _dot(x_ref[...], wa_ref[...])
    cq = h[:, :MLA_Q_LORA]
    ckv = h[:, MLA_Q_LORA:MLA_Q_LORA + MLA_KV_LORA]
    kpe = h[:, MLA_Q_LORA + MLA_KV_LORA:]

    def rms(v, g):
        return (v * lax.rsqrt(jnp.mean(v * v, axis=-1, keepdims=True) + 1e-6) * g).astype(BF16)

    cqn = rms(cq, gq_ref[...])
    ckvn = rms(ckv, gkv_ref[...])
    cos = cos_ref[...]
    sin = sin_ref[...]
    kpe_rot = _rope_rotate(kpe, cos, sin, _x1_mask_mla(tm), MLA_ROPE_HALF)

    q = _dot(cqn, wuq_ref[...])
    q_partner = _dot(cqn, wuqp_ref[...])
    kn = _dot(ckvn, wuk_ref[...])
    cos_q = cos * scale
    sin_q = sin * scale
    for hd in range(MLA_HEADS):
        sl = slice(hd * LANE, (hd + 1) * LANE)
        q_ref[:, sl] = (q[:, sl] * cos_q + q_partner[:, sl] * sin_q).astype(BF16)
        k_ref[:, sl] = (kn[:, sl] + kpe_rot).astype(BF16)
    v = _dot(ckvn, wuv_ref[...]).astype(BF16)
    ones = jnp.ones((tm, LANE), BF16)
    for pair in range(MLA_HEADS // 2):
        v_ref[:, 2 * pair * LANE:(2 * pair + 1) * LANE] = v[:, pair * LANE:(pair + 1) * LANE]
        v_ref[:, (2 * pair + 1) * LANE:(2 * pair + 2) * LANE] = ones


def _layer_block(stack, layer):
    return pl.BlockSpec((None,) + stack.shape[1:], lambda *_: (layer, 0, 0), pipeline_mode=pl.Buffered(1))


def _mla_proj(xb, wa, gq, gkv, wuq, wuqp, wuk, wuv, cos, sin, layer, tm=1024):
    t = xb.shape[0]
    na = wa.shape[2]
    nq = MLA_HEADS * LANE
    nv = MLA_HEADS * LANE
    rows = lambda n: pl.BlockSpec((tm, n), lambda i: (i, 0))
    stacks = (wa, gq, gkv, wuq, wuqp, wuk, wuv)
    vmem = _vmem_estimate(
        [((tm, D_MODEL), BF16)] + [(a.shape[1:], BF16) for a in (wa, wuq, wuqp, wuk, wuv)]
        + [((tm, LANE), F32), ((tm, LANE), F32), ((tm, nq), BF16), ((tm, nq), BF16), ((tm, nv), BF16)],
        [((tm, na), F32)] + [((tm, nq), F32)] * 4)
    return pl.pallas_call(
        functools.partial(_mla_proj_kernel, scale=LOG2E * (MLA_NOPE + MLA_ROPE) ** -0.5),
        grid=(t // tm,),
        in_specs=[rows(D_MODEL)] + [_layer_block(a, layer) for a in stacks] + [rows(LANE), rows(LANE)],
        out_specs=[rows(nq), rows(nq), rows(nv)],
        out_shape=[jax.ShapeDtypeStruct((t, nq), BF16), jax.ShapeDtypeStruct((t, nq), BF16),
                   jax.ShapeDtypeStruct((t, nv), BF16)],
        compiler_params=_compiler_params(("parallel",), vmem),
        name="mla_proj",
    )(xb, *stacks, cos, sin)


def _branch_proj_kernel(x_ref, w_ref, cos_ref, sin_ref, qd_ref, kd_ref, vd_ref, qe_ref, *, q_scale, mem_scale):
    tm = x_ref.shape[0]
    x = x_ref[...]
    cos = cos_ref[...]
    sin = sin_ref[...]
    x1 = _x1_mask_diff(tm)

    def rotated(y, scale, o_ref):
        for hd in range(DIFF_HEADS):
            sl = slice(hd * LANE, (hd + 1) * LANE)
            yh = _rope_rotate(y[:, sl], cos, sin, x1, DIFF_ROPE_HALF)
            o_ref[:, sl] = (yh if scale == 1.0 else yh * scale).astype(o_ref.dtype)

    rotated(_dot(x, w_ref[:, :DIFF_QK]), q_scale, qd_ref)
    rotated(_dot(x, w_ref[:, DIFF_QK:2 * DIFF_QK]), 1.0, kd_ref)
    v = _dot(x, w_ref[:, 2 * DIFF_QK:2 * DIFF_QK + DIFF_OUT]).astype(vd_ref.dtype)
    ones = jnp.ones((tm, LANE), vd_ref.dtype)
    for hd in range(DIFF_HEADS):
        vd_ref[:, 2 * hd * LANE:(2 * hd + 1) * LANE] = v[:, hd * LANE:(hd + 1) * LANE]
        vd_ref[:, (2 * hd + 1) * LANE:(2 * hd + 2) * LANE] = ones
    qe_ref[...] = (_dot(x, w_ref[:, 2 * DIFF_QK + DIFF_OUT:]) * mem_scale).astype(qe_ref.dtype)


def _branch_proj(xb, w_stack, cos, sin, layer, tm=1024):
    t, kdim = xb.shape
    n = w_stack.shape[2]
    rows = lambda width: pl.BlockSpec((tm, width), lambda i: (i, 0))
    widths = (DIFF_QK, DIFF_QK, 2 * DIFF_OUT, MEM_OUT)
    vmem = _vmem_estimate(
        [((tm, kdim), BF16), ((kdim, n), BF16), ((tm, LANE), F32), ((tm, LANE), F32)]
        + [((tm, width), BF16) for width in widths],
        [((tm, DIFF_QK), F32)] * 6)
    return pl.pallas_call(
        functools.partial(_branch_proj_kernel, q_scale=LOG2E * DIFF_D ** -0.5, mem_scale=MEM_HD ** -0.5),
        grid=(t // tm,),
        in_specs=[rows(kdim), _layer_block(w_stack, layer), rows(LANE), rows(LANE)],
        out_specs=[rows(width) for width in widths],
        out_shape=[jax.ShapeDtypeStruct((t, width), BF16) for width in widths],
        compiler_params=_compiler_params(("parallel",), vmem),
        name="branch_proj",
    )(xb, w_stack, cos, sin)


def _mem_kv_kernel(x_ref, w_ref, o_ref):
    o_ref[...] = _dot(x_ref[...], w_ref[...]).astype(o_ref.dtype)


def _mem_kv_proj(memb, w_stack, layer, tm):
    t, kdim = memb.shape
    n = w_stack.shape[2]
    vmem = _vmem_estimate([((tm, kdim), BF16), ((kdim, n), BF16), ((tm, n), BF16)], [((tm, n), F32)] * 2)
    return pl.pallas_call(
        _mem_kv_kernel,
        grid=(t // tm,),
        in_specs=[pl.BlockSpec((tm, kdim), lambda i: (i, 0)),
                  _layer_block(w_stack, layer)],
        out_specs=pl.BlockSpec((tm, n), lambda i: (i, 0)),
        out_shape=jax.ShapeDtypeStruct((t, n), BF16),
        compiler_params=_compiler_params(("parallel",), vmem),
        name="mem_kv_proj",
    )(memb, w_stack)


def _softmax_pv(q, k, v_ext):
    return _normalized_pv(_softmax_numerators(q, k), v_ext)


def _softmax_numerators(q, k):
    return _numerators_from_scores(_dot_nt(q, k))


def _numerators_from_scores(s):
    m = jnp.max(s, axis=-1, keepdims=True)
    return jnp.exp2((s - m).astype(BF16))


def _normalized_pv(p, v_ext):
    pv = _dot(p, v_ext)
    return pv[:, :LANE] / pv[:, LANE:]


ATTN_ROWS = 256


def _mla_attn_kernel(q_ref, k_ref, v_ref, o_ref, *, pairs):
    first = lax.broadcasted_iota(jnp.int32, (ATTN_ROWS, LANE), 1) < MLA_V
    for sub in range(q_ref.shape[0] // ATTN_ROWS):
        rows = slice(sub * ATTN_ROWS, (sub + 1) * ATTN_ROWS)
        for pr in range(pairs):
            v_ext = v_ref[:, 2 * pr * LANE:(2 * pr + 2) * LANE]
            outs = []
            for j in range(2):
                sl = slice((2 * pr + j) * LANE, (2 * pr + j + 1) * LANE)
                outs.append(_softmax_pv(q_ref[rows, sl], k_ref[:, sl], v_ext))
            o_ref[rows, pr * LANE:(pr + 1) * LANE] = jnp.where(first, outs[0], outs[1]).astype(o_ref.dtype)


def _mla_attn(q, k, v, batch, seq, tq=1024, pairs=4):
    t = q.shape[0]
    nq = seq // tq
    wide = 2 * pairs * LANE
    grid = (batch, MLA_HEADS // (2 * pairs), nq)
    vmem = _vmem_estimate(
        [((tq, wide), BF16), ((seq, wide), BF16), ((seq, wide), BF16), ((tq, pairs * LANE), BF16)],
        ([((tq, seq), F32)] * 2 + [((tq, seq), BF16)] * 2) * 2 * pairs)
    return pl.pallas_call(
        functools.partial(_mla_attn_kernel, pairs=pairs),
        grid=grid,
        in_specs=[
            pl.BlockSpec((tq, wide), lambda b, g, i: (b * nq + i, g)),
            pl.BlockSpec((seq, wide), lambda b, g, i: (b, g)),
            pl.BlockSpec((seq, wide), lambda b, g, i: (b, g)),
        ],
        out_specs=pl.BlockSpec((tq, pairs * LANE), lambda b, g, i: (b * nq + i, g)),
        out_shape=jax.ShapeDtypeStruct((t, MLA_OUT), BF16),
        compiler_params=_compiler_params(("parallel", "parallel", "arbitrary"), vmem),
        name="mla_attn",
    )(q, k, v)


def _diff_attn_kernel(q_ref, k_ref, v_ref, lam_ref, g_ref, o_ref, *, lambda_init, heads):
    tq = q_ref.shape[0]
    lam = lam_ref[...]
    lam_a = jnp.sum(lam[0:1] * lam[1:2], axis=-1, keepdims=True)
    lam_b = jnp.sum(lam[2:3] * lam[3:4], axis=-1, keepdims=True)
    lambda_full = jnp.exp(lam_a) - jnp.exp(lam_b) + lambda_init
    map0 = lax.broadcasted_iota(jnp.int32, (ATTN_ROWS, LANE), 1) < DIFF_D
    gain = g_ref[...] * (1.0 - lambda_init)
    for sub in range(tq // ATTN_ROWS):
        rows = slice(sub * ATTN_ROWS, (sub + 1) * ATTN_ROWS)
        for hd in range(heads):
            sl = slice(hd * LANE, (hd + 1) * LANE)
            q = q_ref[rows, sl]
            k = k_ref[:, sl]
            v_ext = v_ref[:, 2 * hd * LANE:(2 * hd + 2) * LANE]
            zero = jnp.zeros_like(q)
            o = _softmax_pv(jnp.where(map0, q, zero), k, v_ext) - lambda_full * _softmax_pv(
                jnp.where(map0, zero, q), k, v_ext)
            o = o * lax.rsqrt(jnp.mean(o * o, axis=-1, keepdims=True) + 1e-5) * gain
            o_ref[rows, sl] = o.astype(o_ref.dtype)


def _diff_attn(q, k, v, lam_stack, subln_stack, layer, lambda_init, batch, seq, tq=512, heads=8):
    t = q.shape[0]
    nq = seq // tq
    wide = heads * LANE
    vmem = _vmem_estimate(
        [((tq, wide), BF16), ((seq, wide), BF16), ((seq, 2 * wide), BF16), ((tq, wide), BF16)],
        ([((tq, seq), F32)] * 2 + [((tq, seq), BF16)] * 2) * 2 * heads)
    return pl.pallas_call(
        functools.partial(_diff_attn_kernel, lambda_init=lambda_init, heads=heads),
        grid=(batch, DIFF_HEADS // heads, nq),
        in_specs=[
            pl.BlockSpec((tq, wide), lambda b, h, i: (b * nq + i, h)),
            pl.BlockSpec((seq, wide), lambda b, h, i: (b, h)),
            pl.BlockSpec((seq, 2 * wide), lambda b, h, i: (b, h)),
            _layer_block(lam_stack, layer),
            _layer_block(subln_stack, layer),
        ],
        out_specs=pl.BlockSpec((tq, wide), lambda b, h, i: (b * nq + i, h)),
        out_shape=jax.ShapeDtypeStruct((t, DIFF_OUT), BF16),
        compiler_params=_compiler_params(("parallel", "parallel", "arbitrary"), vmem),
        name="diff_attn",
    )(q, k, v, lam_stack, subln_stack)


def _mem_attn_kernel(q_ref, kv_ref, o_ref):
    for hd in range(MEM_HEADS):
        sl = slice(hd * MEM_HD, (hd + 1) * MEM_HD)
        vsl = slice(MEM_OUT + hd * MEM_HD, MEM_OUT + (hd + 1) * MEM_HD)
        s = _dot_nt(q_ref[:, sl], kv_ref[:, sl])
        p = jnp.exp(s - jnp.max(s, axis=-1, keepdims=True))
        inv_l = 1.0 / jnp.sum(p, axis=-1, keepdims=True)
        o_ref[:, sl] = (_dot(p.astype(BF16), kv_ref[:, vsl]) * inv_l).astype(o_ref.dtype)


def _mem_attn(q, kv, batch, seq, mem_len, tq=2048):
    t = q.shape[0]
    nq = seq // tq
    vmem = _vmem_estimate(
        [((tq, MEM_OUT), BF16), ((mem_len, 2 * MEM_OUT), BF16), ((tq, MEM_OUT), BF16)],
        [((tq, mem_len), F32)] * 3 + [((tq, MEM_OUT), F32)])
    return pl.pallas_call(
        _mem_attn_kernel,
        grid=(batch, nq),
        in_specs=[
            pl.BlockSpec((tq, MEM_OUT), lambda b, i: (b * nq + i, 0)),
            pl.BlockSpec((mem_len, 2 * MEM_OUT), lambda b, i: (b, 0)),
        ],
        out_specs=pl.BlockSpec((tq, MEM_OUT), lambda b, i: (b * nq + i, 0)),
        out_shape=jax.ShapeDtypeStruct((t, MEM_OUT), BF16),
        compiler_params=_compiler_params(("parallel", "arbitrary"), vmem),
        name="mem_attn",
    )(q, kv)


MERGE_ROW_CHUNK = 256


def _merge_kernel(*refs, pieces):
    n_in = sum(pieces)
    x_ref, xb_ref = refs[:2]
    branch_refs = refs[2:2 + n_in]
    wg_ref, bg_ref, wb_ref, wo_ref, g_ref, b_ref, y_ref, yb_ref = refs[2 + n_in:]
    chunk = MERGE_ROW_CHUNK
    for c in range(x_ref.shape[0] // chunk):
        rows = slice(c * chunk, (c + 1) * chunk)
        xb = xb_ref[rows, :]
        merged = None
        row = 0
        piece = 0
        for i, count in enumerate(pieces):
            sl = slice(i * D_MODEL, (i + 1) * D_MODEL)
            gate = jax.nn.sigmoid(_dot(xb, wg_ref[:, sl]) + bg_ref[i:i + 1, :])
            proj = None
            for o_ref in branch_refs[piece:piece + count]:
                width = o_ref.shape[1]
                part = _dot(o_ref[rows, :], wb_ref[row:row + width, :])
                proj = part if proj is None else proj + part
                row += width
            piece += count
            merged = gate * proj if merged is None else merged + gate * proj
        z = DEEPNORM_ALPHA * x_ref[rows, :] + _dot(merged.astype(BF16), wo_ref[...])
        y = _layernorm(z, g_ref[...], b_ref[...])
        y_ref[rows, :] = y
        yb_ref[rows, :] = y.astype(BF16)


def _merge(x, xb, branches, wg_stack, bg_stack, wb_stack, wo_stack, g_stack, b_stack, layer, tm=1024):
    t = x.shape[0]
    rows = lambda n: pl.BlockSpec((tm, n), lambda i: (i, 0))
    flat = [a for group in branches for a in group]
    vmem = _vmem_estimate(
        [((tm, D_MODEL), F32), ((tm, D_MODEL), BF16)] + [((tm, a.shape[1]), BF16) for a in flat]
        + [(wg_stack.shape[1:], BF16), (wb_stack.shape[1:], BF16), (wo_stack.shape[1:], BF16),
           ((tm, D_MODEL), F32), ((tm, D_MODEL), BF16)],
        [((tm, D_MODEL), F32)] * 5)
    return pl.pallas_call(
        functools.partial(_merge_kernel, pieces=tuple(len(group) for group in branches)),
        grid=(t // tm,),
        in_specs=[rows(D_MODEL), rows(D_MODEL)] + [rows(a.shape[1]) for a in flat]
        + [_layer_block(wg_stack, layer), _layer_block(bg_stack, layer),
           _layer_block(wb_stack, layer), _layer_block(wo_stack, layer),
           _layer_block(g_stack, layer), _layer_block(b_stack, layer)],
        out_specs=[rows(D_MODEL), rows(D_MODEL)],
        out_shape=[jax.ShapeDtypeStruct((t, D_MODEL), F32), jax.ShapeDtypeStruct((t, D_MODEL), BF16)],
        compiler_params=_compiler_params(("parallel",), vmem),
        name="merge",
    )(x, xb, *flat, wg_stack, bg_stack, wb_stack, wo_stack, g_stack, b_stack)


MLP_ROW_CHUNK = 256
MLP_HIDDEN_CHUNK = 1024


def _mlp_kernel(x_ref, xb_ref, w1_ref, w2_ref, g_ref, b_ref, y_ref, yb_ref):
    for c in range(x_ref.shape[0] // MLP_ROW_CHUNK):
        rows = slice(c * MLP_ROW_CHUNK, (c + 1) * MLP_ROW_CHUNK)
        xb = xb_ref[rows, :]
        acc = None
        for f in range(D_FF // MLP_HIDDEN_CHUNK):
            cols = slice(f * MLP_HIDDEN_CHUNK, (f + 1) * MLP_HIDDEN_CHUNK)
            h = jnp.maximum(_dot(xb, w1_ref[:, cols]), 0.0)
            part = _dot((h * h).astype(BF16), w2_ref[cols, :])
            acc = part if acc is None else acc + part
        y = _layernorm(DEEPNORM_ALPHA * x_ref[rows, :] + acc, g_ref[...], b_ref[...])
        y_ref[rows, :] = y
        yb_ref[rows, :] = y.astype(BF16)


def _mlp(x, xb, w1_stack, w2_stack, g_stack, b_stack, layer, tm=1024):
    t = x.shape[0]
    resident = lambda stack: _layer_block(stack, layer)
    vmem = (_vmem_estimate(
        [((tm, D_MODEL), F32), ((tm, D_MODEL), BF16), ((tm, D_MODEL), F32), ((tm, D_MODEL), BF16)],
        [((MLP_ROW_CHUNK, MLP_HIDDEN_CHUNK), F32)] * 4 + [((MLP_ROW_CHUNK, D_MODEL), F32)] * 4)
        + _nbytes(w1_stack.shape[1:], BF16) + _nbytes(w2_stack.shape[1:], BF16))
    return pl.pallas_call(
        _mlp_kernel,
        grid=(t // tm,),
        in_specs=[
            pl.BlockSpec((tm, D_MODEL), lambda i: (i, 0)),
            pl.BlockSpec((tm, D_MODEL), lambda i: (i, 0)),
            resident(w1_stack),
            resident(w2_stack),
            _layer_block(g_stack, layer),
            _layer_block(b_stack, layer),
        ],
        out_specs=[pl.BlockSpec((tm, D_MODEL), lambda i: (i, 0))] * 2,
        out_shape=[jax.ShapeDtypeStruct((t, D_MODEL), F32), jax.ShapeDtypeStruct((t, D_MODEL), BF16)],
        compiler_params=_compiler_params(("parallel",), vmem),
        name="mlp",
    )(x, xb, w1_stack, w2_stack, g_stack, b_stack)


def _lane_patterns():
    lane = jnp.arange(LANE)
    inv_m = ROPE_THETA ** (-jnp.arange(0, MLA_ROPE, 2, dtype=F32) / MLA_ROPE)
    rel = lane - MLA_ROPE_LO
    in_rope = (rel >= 0) & (rel < MLA_ROPE)
    inv_mla = jnp.where(in_rope, inv_m[jnp.clip(rel, 0, MLA_ROPE - 1) % MLA_ROPE_HALF], 0.0)
    sgn_mla = jnp.where(in_rope, jnp.where(rel < MLA_ROPE_HALF, -1.0, 1.0), 0.0)
    inv_d = ROPE_THETA ** (-jnp.arange(0, DIFF_ROT, 2, dtype=F32) / DIFF_ROT)
    rel = lane % DIFF_D
    in_rope = rel < DIFF_ROT
    inv_diff = jnp.where(in_rope, inv_d[rel % DIFF_ROPE_HALF], 0.0)
    sgn_diff = jnp.where(in_rope, jnp.where(rel < DIFF_ROPE_HALF, -1.0, 1.0), 0.0)
    row = lambda a: a.astype(F32).reshape(1, LANE)
    return row(inv_mla), row(sgn_mla), row(inv_diff), row(sgn_diff)


def _weight_stacks(w_in, mla_w_uq, mla_w_ukv):
    depth, d, _ = w_in.shape
    lat = MLA_Q_LORA + MLA_KV_LORA
    branch_lo = lat + MLA_ROPE
    gate_lo = branch_lo + 2 * DIFF_QK + DIFF_OUT + MEM_OUT
    w_a = jnp.concatenate(
        [w_in[:, :, :lat], jnp.zeros((depth, d, MLA_ROPE_LO), F32), w_in[:, :, lat:branch_lo],
         jnp.zeros((depth, d, LANE - MLA_ROPE_LO - MLA_ROPE), F32)], axis=2).astype(BF16)
    w_branch_in = w_in[:, :, branch_lo:gate_lo].astype(BF16)
    w_gate = w_in[:, :, gate_lo:].astype(BF16)

    qd = MLA_NOPE + MLA_ROPE
    w_uq = jnp.pad(mla_w_uq.reshape(depth, MLA_Q_LORA, MLA_HEADS, qd), ((0, 0), (0, 0), (0, 0), (0, LANE - qd)))
    lane = jnp.arange(LANE)
    rel = lane - MLA_ROPE_LO
    in_rope = (rel >= 0) & (rel < MLA_ROPE)
    partner = jnp.where(rel < MLA_ROPE_HALF, lane + MLA_ROPE_HALF, lane - MLA_ROPE_HALF)
    w_uq_partner = jnp.where(in_rope, jnp.take(w_uq, jnp.clip(partner, 0, LANE - 1), axis=3), 0.0)
    flat = lambda a: a.reshape(depth, a.shape[1], -1).astype(BF16)
    ukv = mla_w_ukv.reshape(depth, MLA_KV_LORA, MLA_HEADS, MLA_NOPE + MLA_V)
    w_uk = jnp.pad(ukv[..., :MLA_NOPE], ((0, 0), (0, 0), (0, 0), (0, LANE - MLA_NOPE)))
    return w_a, w_branch_in, w_gate, flat(w_uq), flat(w_uq_partner), flat(w_uk), flat(ukv[..., MLA_NOPE:])


def kernel(x, mem, positions, w_in, b_gate, mla_q_norm, mla_kv_norm, mla_w_uq, mla_w_ukv, diff_lambda, diff_subln,
           mem_w_kv, w_branch, w_out, ln1_g, ln1_b, mlp_w1, mlp_w2, ln2_g, ln2_b):
    batch, seq, d = x.shape
    mem_len = mem.shape[1]
    t = batch * seq
    depth = w_in.shape[0]

    inv_mla, sgn_mla, inv_diff, sgn_diff = _lane_patterns()
    pos = positions.reshape(t, 1)
    cos_m, sin_m = _rope_tables(pos, inv_mla, sgn_mla)
    cos_d, sin_d = _rope_tables(pos, inv_diff, sgn_diff)

    xf = x.reshape(t, d)
    xb = xf.astype(BF16)
    memb = mem.reshape(batch * mem_len, d).astype(BF16)

    w_a, w_branch_in, w_gate, w_uq, w_uq_partner, w_uk, w_uv = _weight_stacks(w_in, mla_w_uq, mla_w_ukv)
    mem_w_kv_b, w_branch_b, w_out_b, w1_b, w2_b = (
        a.astype(BF16) for a in (mem_w_kv, w_branch, w_out, mlp_w1, mlp_w2))
    stack_rows = lambda a: a.reshape(depth, 1, -1)
    q_norm_s, kv_norm_s, subln_s, ln1_g_s, ln1_b_s, ln2_g_s, ln2_b_s = map(
        stack_rows, (mla_q_norm, mla_kv_norm, diff_subln, ln1_g, ln1_b, ln2_g, ln2_b))

    for l in range(depth):
        lambda_init = 0.8 - 0.6 * math.exp(-0.3 * l)
        q_m, k_m, v_m = _mla_proj(xb, w_a, q_norm_s, kv_norm_s, w_uq, w_uq_partner, w_uk, w_uv, cos_m, sin_m, l)
        o_mla = _mla_attn(q_m, k_m, v_m, batch, seq)

        q_d, k_d, v_d, q_e = _branch_proj(xb, w_branch_in, cos_d, sin_d, l)
        o_diff = _diff_attn(q_d, k_d, v_d, diff_lambda, subln_s, l, lambda_init, batch, seq)

        kv_e = _mem_kv_proj(memb, mem_w_kv_b, l, tm=mem_len)
        o_mem = _mem_attn(q_e, kv_e, batch, seq, mem_len)

        xf, xb = _merge(xf, xb, ((o_mla,), (o_diff,), (o_mem,)), w_gate, b_gate, w_branch_b, w_out_b,
                        ln1_g_s, ln1_b_s, l)
        xf, xb = _mlp(xf, xb, w1_b, w2_b, ln2_g_s, ln2_b_s, l)
    return xf.reshape(batch, seq, d)
```

```python
import functools
import math

import jax
import jax.numpy as jnp
from jax import lax
from jax.experimental import pallas as pl
from jax.experimental.pallas import tpu as pltpu

D_MODEL = 1024
DEPTH = 4
ROPE_THETA = 500000.0
MLA_HEADS = 8
MLA_NOPE = 64
MLA_ROPE = 32
MLA_V = 64
MLA_Q_LORA = 384
MLA_KV_LORA = 256
DIFF_HEADS = 8
DIFF_D = 64
DIFF_ROT = DIFF_D // 4
MEM_HEADS = 4
MEM_HD = 128
N_BRANCH = 3
D_FF = 4 * D_MODEL
DEEPNORM_ALPHA = (2 * DEPTH) ** 0.25
LN_EPS = 1e-5
MLA_OUT = MLA_HEADS * MLA_V
DIFF_QK = DIFF_HEADS * 2 * DIFF_D
DIFF_OUT = DIFF_HEADS * 2 * DIFF_D
MEM_OUT = MEM_HEADS * MEM_HD

LANE = 128
V7X_VMEM_BYTES = 64 * 1024 * 1024

MLA_ROPE_LO = MLA_NOPE
MLA_ROPE_HALF = MLA_ROPE // 2
DIFF_ROPE_HALF = DIFF_ROT // 2

BF16 = jnp.bfloat16
F32 = jnp.float32
LOG2E = math.log2(math.e)


def _dot(a, b):
    return jnp.dot(a, b, preferred_element_type=F32)


def _dot_nt(a, b):
    return lax.dot_general(a, b, (((1,), (1,)), ((), ())), preferred_element_type=F32)


def _compiler_params(semantics, vmem_bytes):
    limit = min(int(vmem_bytes), V7X_VMEM_BYTES - 8 * 1024 * 1024)
    return pltpu.CompilerParams(dimension_semantics=semantics, vmem_limit_bytes=limit)


def _nbytes(shape, dtype):
    return math.prod(shape) * jnp.dtype(dtype).itemsize


def _vmem_estimate(blocks, temps):
    return 2 * sum(_nbytes(s, d) for s, d in blocks) + sum(_nbytes(s, d) for s, d in temps) + (4 << 20)


def _rope_rotate(y, cos, sin_signed, x1_mask, half):
    partner = jnp.where(x1_mask, pltpu.roll(y, LANE - half, 1), pltpu.roll(y, half, 1))
    return y * cos + partner * sin_signed


def _x1_mask_mla(rows):
    lane = lax.broadcasted_iota(jnp.int32, (rows, LANE), 1)
    return (lane >= MLA_ROPE_LO) & (lane < MLA_ROPE_LO + MLA_ROPE_HALF)


def _x1_mask_diff(rows):
    lane = lax.broadcasted_iota(jnp.int32, (rows, LANE), 1) % DIFF_D
    return lane < DIFF_ROPE_HALF


def _layernorm(z, g, b):
    mu = jnp.mean(z, axis=-1, keepdims=True)
    zc = z - mu
    var = jnp.mean(zc * zc, axis=-1, keepdims=True)
    return zc * lax.rsqrt(var + LN_EPS) * g + b


def _rope_table_kernel(pos_ref, inv_ref, sgn_ref, cos_ref, sin_ref):
    ang = pos_ref[...].astype(F32) * inv_ref[...]
    cos_ref[...] = jnp.cos(ang)
    sin_ref[...] = jnp.sin(ang) * sgn_ref[...]


def _rope_tables(pos, inv_lane, sgn_lane, tm=2048):
    t = pos.shape[0]
    return pl.pallas_call(
        _rope_table_kernel,
        grid=(t // tm,),
        in_specs=[
            pl.BlockSpec((tm, 1), lambda i: (i, 0)),
            pl.BlockSpec((1, LANE), lambda i: (0, 0)),
            pl.BlockSpec((1, LANE), lambda i: (0, 0)),
        ],
        out_specs=[pl.BlockSpec((tm, LANE), lambda i: (i, 0))] * 2,
        out_shape=[jax.ShapeDtypeStruct((t, LANE), F32)] * 2,
        compiler_params=_compiler_params(("parallel",), 32 << 20),
        name="rope_tables",
    )(pos, inv_lane, sgn_lane)


def _mla_proj_kernel(x_ref, wa_ref, gq_ref, gkv_ref, wuq_ref, wuqp_ref, wuk_ref, wuv_ref, cos_ref, sin_ref,
                     q_ref, k_ref, v_ref, *, scale):
    tm = x_ref.shape[0]
    h = _dot(x_ref[...], wa_ref[...])
    cq = h[:, :MLA_Q_LORA]
    ckv = h[:, MLA_Q_LORA:MLA_Q_LORA + MLA_KV_LORA]
    kpe = h[:, MLA_Q_LORA + MLA_KV_LORA:]

    def rms(v, g):
        return (v * lax.rsqrt(jnp.mean(v * v, axis=-1, keepdims=True) + 1e-6) * g).astype(BF16)

    cqn = rms(cq, gq_ref[...])
    ckvn = rms(ckv, gkv_ref[...])
    cos = cos_ref[...]
    sin = sin_ref[...]
    kpe_rot = _rope_rotate(kpe, cos, sin, _x1_mask_mla(tm), MLA_ROPE_HALF)

    q = _dot(cqn, wuq_ref[...])
    q_partner = _dot(cqn, wuqp_ref[...])
    kn = _dot(ckvn, wuk_ref[...])
    cos_q = cos * scale
    sin_q = sin * scale
    for hd in range(MLA_HEADS):
        sl = slice(hd * LANE, (hd + 1) * LANE)
        q_ref[:, sl] = (q[:, sl] * cos_q + q_partner[:, sl] * sin_q).astype(BF16)
        k_ref[:, sl] = (kn[:, sl] + kpe_rot).astype(BF16)
    v = _dot(ckvn, wuv_ref[...]).astype(BF16)
    ones = jnp.ones((tm, LANE), BF16)
    for pair in range(MLA_HEADS // 2):
        v_ref[:, 2 * pair * LANE:(2 * pair + 1) * LANE] = v[:, pair * LANE:(pair + 1) * LANE]
        v_ref[:, (2 * pair + 1) * LANE:(2 * pair + 2) * LANE] = ones


def _layer_block(stack, layer):
    return pl.BlockSpec((None,) + stack.shape[1:], lambda *_: (layer, 0, 0), pipeline_mode=pl.Buffered(1))


def _mla_proj(xb, wa, gq, gkv, wuq, wuqp, wuk, wuv, cos, sin, layer, tm=1024):
    t = xb.shape[0]
    na = wa.shape[2]
    nq = MLA_HEADS * LANE
    nv = MLA_HEADS * LANE
    rows = lambda n: pl.BlockSpec((tm, n), lambda i: (i, 0))
    stacks = (wa, gq, gkv, wuq, wuqp, wuk, wuv)
    vmem = _vmem_estimate(
        [((tm, D_MODEL), BF16)] + [(a.shape[1:], BF16) for a in (wa, wuq, wuqp, wuk, wuv)]
        + [((tm, LANE), F32), ((tm, LANE), F32), ((tm, nq), BF16), ((tm, nq), BF16), ((tm, nv), BF16)],
        [((tm, na), F32)] + [((tm, nq), F32)] * 4)
    return pl.pallas_call(
        functools.partial(_mla_proj_kernel, scale=LOG2E * (MLA_NOPE + MLA_ROPE) ** -0.5),
        grid=(t // tm,),
        in_specs=[rows(D_MODEL)] + [_layer_block(a, layer) for a in stacks] + [rows(LANE), rows(LANE)],
        out_specs=[rows(nq), rows(nq), rows(nv)],
        out_shape=[jax.ShapeDtypeStruct((t, nq), BF16), jax.ShapeDtypeStruct((t, nq), BF16),
                   jax.ShapeDtypeStruct((t, nv), BF16)],
        compiler_params=_compiler_params(("parallel",), vmem),
        name="mla_proj",
    )(xb, *stacks, cos, sin)


def _branch_proj_kernel(x_ref, w_ref, cos_ref, sin_ref, qd_ref, kd_ref, vd_ref, qe_ref, *, q_scale, mem_scale):
    tm = x_ref.shape[0]
    x = x_ref[...]
    cos = cos_ref[...]
    sin = sin_ref[...]
    x1 = _x1_mask_diff(tm)

    def rotated(y, scale, o_ref):
        for hd in range(DIFF_HEADS):
            sl = slice(hd * LANE, (hd + 1) * LANE)
            yh = _rope_rotate(y[:, sl], cos, sin, x1, DIFF_ROPE_HALF)
            o_ref[:, sl] = (yh if scale == 1.0 else yh * scale).astype(o_ref.dtype)

    rotated(_dot(x, w_ref[:, :DIFF_QK]), q_scale, qd_ref)
    rotated(_dot(x, w_ref[:, DIFF_QK:2 * DIFF_QK]), 1.0, kd_ref)
    v = _dot(x, w_ref[:, 2 * DIFF_QK:2 * DIFF_QK + DIFF_OUT]).astype(vd_ref.dtype)
    ones = jnp.ones((tm, LANE), vd_ref.dtype)
    for hd in range(DIFF_HEADS):
        vd_ref[:, 2 * hd * LANE:(2 * hd + 1) * LANE] = v[:, hd * LANE:(hd + 1) * LANE]
        vd_ref[:, (2 * hd + 1) * LANE:(2 * hd + 2) * LANE] = ones
    qe_ref[...] = (_dot(x, w_ref[:, 2 * DIFF_QK + DIFF_OUT:]) * mem_scale).astype(qe_ref.dtype)


def _branch_proj(xb, w_stack, cos, sin, layer, tm=1024):
    t, kdim = xb.shape
    n = w_stack.shape[2]
    rows = lambda width: pl.BlockSpec((tm, width), lambda i: (i, 0))
    widths = (DIFF_QK, DIFF_QK, 2 * DIFF_OUT, MEM_OUT)
    vmem = _vmem_estimate(
        [((tm, kdim), BF16), ((kdim, n), BF16), ((tm, LANE), F32), ((tm, LANE), F32)]
        + [((tm, width), BF16) for width in widths],
        [((tm, DIFF_QK), F32)] * 6)
    return pl.pallas_call(
        functools.partial(_branch_proj_kernel, q_scale=LOG2E * DIFF_D ** -0.5, mem_scale=MEM_HD ** -0.5),
        grid=(t // tm,),
        in_specs=[rows(kdim), _layer_block(w_stack, layer), rows(LANE), rows(LANE)],
        out_specs=[rows(width) for width in widths],
        out_shape=[jax.ShapeDtypeStruct((t, width), BF16) for width in widths],
        compiler_params=_compiler_params(("parallel",), vmem),
        name="branch_proj",
    )(xb, w_stack, cos, sin)


def _mem_kv_kernel(x_ref, w_ref, o_ref):
    o_ref[...] = _dot(x_ref[...], w_ref[...]).astype(o_ref.dtype)


def _mem_kv_proj(memb, w_stack, layer, tm):
    t, kdim = memb.shape
    n = w_stack.shape[2]
    vmem = _vmem_estimate([((tm, kdim), BF16), ((kdim, n), BF16), ((tm, n), BF16)], [((tm, n), F32)] * 2)
    return pl.pallas_call(
        _mem_kv_kernel,
        grid=(t // tm,),
        in_specs=[pl.BlockSpec((tm, kdim), lambda i: (i, 0)),
                  _layer_block(w_stack, layer)],
        out_specs=pl.BlockSpec((tm, n), lambda i: (i, 0)),
        out_shape=jax.ShapeDtypeStruct((t, n), BF16),
        compiler_params=_compiler_params(("parallel",), vmem),
        name="mem_kv_proj",
    )(memb, w_stack)


def _softmax_pv(q, k, v_ext):
    block = k.shape[0] // KEY_SPLITS
    m = None
    pv = None
    for j in range(KEY_SPLITS):
        rows = slice(j * block, (j + 1) * block)
        s = _dot_nt(q, k[rows])
        m_new = jnp.max(s, axis=-1, keepdims=True)
        if m is not None:
            m_new = jnp.maximum(m, m_new)
        part = _dot(jnp.exp2((s - m_new).astype(BF16)), v_ext[rows])
        pv = part if pv is None else pv * jnp.exp2(m - m_new) + part
        m = m_new
    return pv[:, :LANE] / pv[:, LANE:]


KEY_SPLITS = 4


def _softmax_numerators(q, k):
    return _numerators_from_scores(_dot_nt(q, k))


def _numerators_from_scores(s):
    m = jnp.max(s, axis=-1, keepdims=True)
    return jnp.exp2((s - m).astype(BF16))


def _normalized_pv(p, v_ext):
    pv = _dot(p, v_ext)
    return pv[:, :LANE] / pv[:, LANE:]


ATTN_ROWS = 256


def _mla_attn_kernel(q_ref, k_ref, v_ref, o_ref, *, pairs):
    first = lax.broadcasted_iota(jnp.int32, (ATTN_ROWS, LANE), 1) < MLA_V
    for sub in range(q_ref.shape[0] // ATTN_ROWS):
        rows = slice(sub * ATTN_ROWS, (sub + 1) * ATTN_ROWS)
        for pr in range(pairs):
            v_ext = v_ref[:, 2 * pr * LANE:(2 * pr + 2) * LANE]
            outs = []
            for j in range(2):
                sl = slice((2 * pr + j) * LANE, (2 * pr + j + 1) * LANE)
                outs.append(_softmax_pv(q_ref[rows, sl], k_ref[:, sl], v_ext))
            o_ref[rows, pr * LANE:(pr + 1) * LANE] = jnp.where(first, outs[0], outs[1]).astype(o_ref.dtype)


def _mla_attn(q, k, v, batch, seq, tq=1024, pairs=4):
    t = q.shape[0]
    nq = seq // tq
    wide = 2 * pairs * LANE
    grid = (batch, MLA_HEADS // (2 * pairs), nq)
    vmem = _vmem_estimate(
        [((tq, wide), BF16), ((seq, wide), BF16), ((seq, wide), BF16), ((tq, pairs * LANE), BF16)],
        ([((tq, seq), F32)] * 2 + [((tq, seq), BF16)] * 2) * 2 * pairs)
    return pl.pallas_call(
        functools.partial(_mla_attn_kernel, pairs=pairs),
        grid=grid,
        in_specs=[
            pl.BlockSpec((tq, wide), lambda b, g, i: (b * nq + i, g)),
            pl.BlockSpec((seq, wide), lambda b, g, i: (b, g)),
            pl.BlockSpec((seq, wide), lambda b, g, i: (b, g)),
        ],
        out_specs=pl.BlockSpec((tq, pairs * LANE), lambda b, g, i: (b * nq + i, g)),
        out_shape=jax.ShapeDtypeStruct((t, MLA_OUT), BF16),
        compiler_params=_compiler_params(("parallel", "parallel", "arbitrary"), vmem),
        name="mla_attn",
    )(q, k, v)


def _diff_attn_kernel(q_ref, k_ref, v_ref, lam_ref, g_ref, o_ref, *, lambda_init, heads):
    tq = q_ref.shape[0]
    lam = lam_ref[...]
    lam_a = jnp.sum(lam[0:1] * lam[1:2], axis=-1, keepdims=True)
    lam_b = jnp.sum(lam[2:3] * lam[3:4], axis=-1, keepdims=True)
    lambda_full = jnp.exp(lam_a) - jnp.exp(lam_b) + lambda_init
    map0 = lax.broadcasted_iota(jnp.int32, (ATTN_ROWS, LANE), 1) < DIFF_D
    gain = g_ref[...] * (1.0 - lambda_init)
    for sub in range(tq // ATTN_ROWS):
        rows = slice(sub * ATTN_ROWS, (sub + 1) * ATTN_ROWS)
        for hd in range(heads):
            sl = slice(hd * LANE, (hd + 1) * LANE)
            q = q_ref[rows, sl]
            k = k_ref[:, sl]
            v_ext = v_ref[:, 2 * hd * LANE:(2 * hd + 2) * LANE]
            zero = jnp.zeros_like(q)
            o = _softmax_pv(jnp.where(map0, q, zero), k, v_ext) - lambda_full * _softmax_pv(
                jnp.where(map0, zero, q), k, v_ext)
            o = o * lax.rsqrt(jnp.mean(o * o, axis=-1, keepdims=True) + 1e-5) * gain
            o_ref[rows, sl] = o.astype(o_ref.dtype)


def _diff_attn(q, k, v, lam_stack, subln_stack, layer, lambda_init, batch, seq, tq=512, heads=8):
    t = q.shape[0]
    nq = seq // tq
    wide = heads * LANE
    vmem = _vmem_estimate(
        [((tq, wide), BF16), ((seq, wide), BF16), ((seq, 2 * wide), BF16), ((tq, wide), BF16)],
        ([((tq, seq), F32)] * 2 + [((tq, seq), BF16)] * 2) * 2 * heads)
    return pl.pallas_call(
        functools.partial(_diff_attn_kernel, lambda_init=lambda_init, heads=heads),
        grid=(batch, DIFF_HEADS // heads, nq),
        in_specs=[
            pl.BlockSpec((tq, wide), lambda b, h, i: (b * nq + i, h)),
            pl.BlockSpec((seq, wide), lambda b, h, i: (b, h)),
            pl.BlockSpec((seq, 2 * wide), lambda b, h, i: (b, h)),
            _layer_block(lam_stack, layer),
            _layer_block(subln_stack, layer),
        ],
        out_specs=pl.BlockSpec((tq, wide), lambda b, h, i: (b * nq + i, h)),
        out_shape=jax.ShapeDtypeStruct((t, DIFF_OUT), BF16),
        compiler_params=_compiler_params(("parallel", "parallel", "arbitrary"), vmem),
        name="diff_attn",
    )(q, k, v, lam_stack, subln_stack)


def _mem_attn_kernel(q_ref, kv_ref, o_ref):
    for hd in range(MEM_HEADS):
        sl = slice(hd * MEM_HD, (hd + 1) * MEM_HD)
        vsl = slice(MEM_OUT + hd * MEM_HD, MEM_OUT + (hd + 1) * MEM_HD)
        s = _dot_nt(q_ref[:, sl], kv_ref[:, sl])
        p = jnp.exp(s - jnp.max(s, axis=-1, keepdims=True))
        inv_l = 1.0 / jnp.sum(p, axis=-1, keepdims=True)
        o_ref[:, sl] = (_dot(p.astype(BF16), kv_ref[:, vsl]) * inv_l).astype(o_ref.dtype)


def _mem_attn(q, kv, batch, seq, mem_len, tq=2048):
    t = q.shape[0]
    nq = seq // tq
    vmem = _vmem_estimate(
        [((tq, MEM_OUT), BF16), ((mem_len, 2 * MEM_OUT), BF16), ((tq, MEM_OUT), BF16)],
        [((tq, mem_len), F32)] * 3 + [((tq, MEM_OUT), F32)])
    return pl.pallas_call(
        _mem_attn_kernel,
        grid=(batch, nq),
        in_specs=[
            pl.BlockSpec((tq, MEM_OUT), lambda b, i: (b * nq + i, 0)),
            pl.BlockSpec((mem_len, 2 * MEM_OUT), lambda b, i: (b, 0)),
        ],
        out_specs=pl.BlockSpec((tq, MEM_OUT), lambda b, i: (b * nq + i, 0)),
        out_shape=jax.ShapeDtypeStruct((t, MEM_OUT), BF16),
        compiler_params=_compiler_params(("parallel", "arbitrary"), vmem),
        name="mem_attn",
    )(q, kv)


MERGE_ROW_CHUNK = 256


def _merge_kernel(*refs, pieces):
    n_in = sum(pieces)
    x_ref, xb_ref = refs[:2]
    branch_refs = refs[2:2 + n_in]
    wg_ref, bg_ref, wb_ref, wo_ref, g_ref, b_ref, y_ref, yb_ref = refs[2 + n_in:]
    chunk = MERGE_ROW_CHUNK
    for c in range(x_ref.shape[0] // chunk):
        rows = slice(c * chunk, (c + 1) * chunk)
        xb = xb_ref[rows, :]
        merged = None
        row = 0
        piece = 0
        for i, count in enumerate(pieces):
            sl = slice(i * D_MODEL, (i + 1) * D_MODEL)
            gate = jax.nn.sigmoid(_dot(xb, wg_ref[:, sl]) + bg_ref[i:i + 1, :])
            proj = None
            for o_ref in branch_refs[piece:piece + count]:
                width = o_ref.shape[1]
                part = _dot(o_ref[rows, :], wb_ref[row:row + width, :])
                proj = part if proj is None else proj + part
                row += width
            piece += count
            merged = gate * proj if merged is None else merged + gate * proj
        z = DEEPNORM_ALPHA * x_ref[rows, :] + _dot(merged.astype(BF16), wo_ref[...])
        y = _layernorm(z, g_ref[...], b_ref[...])
        y_ref[rows, :] = y
        yb_ref[rows, :] = y.astype(BF16)


def _merge(x, xb, branches, wg_stack, bg_stack, wb_stack, wo_stack, g_stack, b_stack, layer, tm=1024):
    t = x.shape[0]
    rows = lambda n: pl.BlockSpec((tm, n), lambda i: (i, 0))
    flat = [a for group in branches for a in group]
    vmem = _vmem_estimate(
        [((tm, D_MODEL), F32), ((tm, D_MODEL), BF16)] + [((tm, a.shape[1]), BF16) for a in flat]
        + [(wg_stack.shape[1:], BF16), (wb_stack.shape[1:], BF16), (wo_stack.shape[1:], BF16),
           ((tm, D_MODEL), F32), ((tm, D_MODEL), BF16)],
        [((tm, D_MODEL), F32)] * 5)
    return pl.pallas_call(
        functools.partial(_merge_kernel, pieces=tuple(len(group) for group in branches)),
        grid=(t // tm,),
        in_specs=[rows(D_MODEL), rows(D_MODEL)] + [rows(a.shape[1]) for a in flat]
        + [_layer_block(wg_stack, layer), _layer_block(bg_stack, layer),
           _layer_block(wb_stack, layer), _layer_block(wo_stack, layer),
           _layer_block(g_stack, layer), _layer_block(b_stack, layer)],
        out_specs=[rows(D_MODEL), rows(D_MODEL)],
        out_shape=[jax.ShapeDtypeStruct((t, D_MODEL), F32), jax.ShapeDtypeStruct((t, D_MODEL), BF16)],
        compiler_params=_compiler_params(("parallel",), vmem),
        name="merge",
    )(x, xb, *flat, wg_stack, bg_stack, wb_stack, wo_stack, g_stack, b_stack)


MLP_ROW_CHUNK = 256
MLP_HIDDEN_CHUNK = 1024


def _mlp_kernel(x_ref, xb_ref, w1_ref, w2_ref, g_ref, b_ref, y_ref, yb_ref):
    for c in range(x_ref.shape[0] // MLP_ROW_CHUNK):
        rows = slice(c * MLP_ROW_CHUNK, (c + 1) * MLP_ROW_CHUNK)
        xb = xb_ref[rows, :]
        acc = None
        for f in range(D_FF // MLP_HIDDEN_CHUNK):
            cols = slice(f * MLP_HIDDEN_CHUNK, (f + 1) * MLP_HIDDEN_CHUNK)
            h = jnp.maximum(_dot(xb, w1_ref[:, cols]), 0.0)
            part = _dot((h * h).astype(BF16), w2_ref[cols, :])
            acc = part if acc is None else acc + part
        y = _layernorm(DEEPNORM_ALPHA * x_ref[rows, :] + acc, g_ref[...], b_ref[...])
        y_ref[rows, :] = y
        yb_ref[rows, :] = y.astype(BF16)


def _mlp(x, xb, w1_stack, w2_stack, g_stack, b_stack, layer, tm=1024):
    t = x.shape[0]
    resident = lambda stack: _layer_block(stack, layer)
    vmem = (_vmem_estimate(
        [((tm, D_MODEL), F32), ((tm, D_MODEL), BF16), ((tm, D_MODEL), F32), ((tm, D_MODEL), BF16)],
        [((MLP_ROW_CHUNK, MLP_HIDDEN_CHUNK), F32)] * 4 + [((MLP_ROW_CHUNK, D_MODEL), F32)] * 4)
        + _nbytes(w1_stack.shape[1:], BF16) + _nbytes(w2_stack.shape[1:], BF16))
    return pl.pallas_call(
        _mlp_kernel,
        grid=(t // tm,),
        in_specs=[
            pl.BlockSpec((tm, D_MODEL), lambda i: (i, 0)),
            pl.BlockSpec((tm, D_MODEL), lambda i: (i, 0)),
            resident(w1_stack),
            resident(w2_stack),
            _layer_block(g_stack, layer),
            _layer_block(b_stack, layer),
        ],
        out_specs=[pl.BlockSpec((tm, D_MODEL), lambda i: (i, 0))] * 2,
        out_shape=[jax.ShapeDtypeStruct((t, D_MODEL), F32), jax.ShapeDtypeStruct((t, D_MODEL), BF16)],
        compiler_params=_compiler_params(("parallel",), vmem),
        name="mlp",
    )(x, xb, w1_stack, w2_stack, g_stack, b_stack)


def _lane_patterns():
    lane = jnp.arange(LANE)
    inv_m = ROPE_THETA ** (-jnp.arange(0, MLA_ROPE, 2, dtype=F32) / MLA_ROPE)
    rel = lane - MLA_ROPE_LO
    in_rope = (rel >= 0) & (rel < MLA_ROPE)
    inv_mla = jnp.where(in_rope, inv_m[jnp.clip(rel, 0, MLA_ROPE - 1) % MLA_ROPE_HALF], 0.0)
    sgn_mla = jnp.where(in_rope, jnp.where(rel < MLA_ROPE_HALF, -1.0, 1.0), 0.0)
    inv_d = ROPE_THETA ** (-jnp.arange(0, DIFF_ROT, 2, dtype=F32) / DIFF_ROT)
    rel = lane % DIFF_D
    in_rope = rel < DIFF_ROT
    inv_diff = jnp.where(in_rope, inv_d[rel % DIFF_ROPE_HALF], 0.0)
    sgn_diff = jnp.where(in_rope, jnp.where(rel < DIFF_ROPE_HALF, -1.0, 1.0), 0.0)
    row = lambda a: a.astype(F32).reshape(1, LANE)
    return row(inv_mla), row(sgn_mla), row(inv_diff), row(sgn_diff)


def _weight_stacks(w_in, mla_w_uq, mla_w_ukv):
    depth, d, _ = w_in.shape
    lat = MLA_Q_LORA + MLA_KV_LORA
    branch_lo = lat + MLA_ROPE
    gate_lo = branch_lo + 2 * DIFF_QK + DIFF_OUT + MEM_OUT
    w_a = jnp.concatenate(
        [w_in[:, :, :lat], jnp.zeros((depth, d, MLA_ROPE_LO), F32), w_in[:, :, lat:branch_lo],
         jnp.zeros((depth, d, LANE - MLA_ROPE_LO - MLA_ROPE), F32)], axis=2).astype(BF16)
    w_branch_in = w_in[:, :, branch_lo:gate_lo].astype(BF16)
    w_gate = w_in[:, :, gate_lo:].astype(BF16)

    qd = MLA_NOPE + MLA_ROPE
    w_uq = jnp.pad(mla_w_uq.reshape(depth, MLA_Q_LORA, MLA_HEADS, qd), ((0, 0), (0, 0), (0, 0), (0, LANE - qd)))
    lane = jnp.arange(LANE)
    rel = lane - MLA_ROPE_LO
    in_rope = (rel >= 0) & (rel < MLA_ROPE)
    partner = jnp.where(rel < MLA_ROPE_HALF, lane + MLA_ROPE_HALF, lane - MLA_ROPE_HALF)
    w_uq_partner = jnp.where(in_rope, jnp.take(w_uq, jnp.clip(partner, 0, LANE - 1), axis=3), 0.0)
    flat = lambda a: a.reshape(depth, a.shape[1], -1).astype(BF16)
    ukv = mla_w_ukv.reshape(depth, MLA_KV_LORA, MLA_HEADS, MLA_NOPE + MLA_V)
    w_uk = jnp.pad(ukv[..., :MLA_NOPE], ((0, 0), (0, 0), (0, 0), (0, LANE - MLA_NOPE)))
    return w_a, w_branch_in, w_gate, flat(w_uq), flat(w_uq_partner), flat(w_uk), flat(ukv[..., MLA_NOPE:])


def kernel(x, mem, positions, w_in, b_gate, mla_q_norm, mla_kv_norm, mla_w_uq, mla_w_ukv, diff_lambda, diff_subln,
           mem_w_kv, w_branch, w_out, ln1_g, ln1_b, mlp_w1, mlp_w2, ln2_g, ln2_b):
    batch, seq, d = x.shape
    mem_len = mem.shape[1]
    t = batch * seq
    depth = w_in.shape[0]

    inv_mla, sgn_mla, inv_diff, sgn_diff = _lane_patterns()
    pos = positions.reshape(t, 1)
    cos_m, sin_m = _rope_tables(pos, inv_mla, sgn_mla)
    cos_d, sin_d = _rope_tables(pos, inv_diff, sgn_diff)

    xf = x.reshape(t, d)
    xb = xf.astype(BF16)
    memb = mem.reshape(batch * mem_len, d).astype(BF16)

    w_a, w_branch_in, w_gate, w_uq, w_uq_partner, w_uk, w_uv = _weight_stacks(w_in, mla_w_uq, mla_w_ukv)
    mem_w_kv_b, w_branch_b, w_out_b, w1_b, w2_b = (
        a.astype(BF16) for a in (mem_w_kv, w_branch, w_out, mlp_w1, mlp_w2))
    stack_rows = lambda a: a.reshape(depth, 1, -1)
    q_norm_s, kv_norm_s, subln_s, ln1_g_s, ln1_b_s, ln2_g_s, ln2_b_s = map(
        stack_rows, (mla_q_norm, mla_kv_norm, diff_subln, ln1_g, ln1_b, ln2_g, ln2_b))

    for l in range(depth):
        lambda_init = 0.8 - 0.6 * math.exp(-0.3 * l)
        q_m, k_m, v_m = _mla_proj(xb, w_a, q_norm_s, kv_norm_s, w_uq, w_uq_partner, w_uk, w_uv, cos_m, sin_m, l)
        o_mla = _mla_attn(q_m, k_m, v_m, batch, seq)

        q_d, k_d, v_d, q_e = _branch_proj(xb, w_branch_in, cos_d, sin_d, l)
        o_diff = _diff_attn(q_d, k_d, v_d, diff_lambda, subln_s, l, lambda_init, batch, seq)

        kv_e = _mem_kv_proj(memb, mem_w_kv_b, l, tm=mem_len)
        o_mem = _mem_attn(q_e, kv_e, batch, seq, mem_len)

        xf, xb = _merge(xf, xb, ((o_mla,), (o_diff,), (o_mem,)), w_gate, b_gate, w_branch_b, w_out_b,
                        ln1_g_s, ln1_b_s, l)
        xf, xb = _mlp(xf, xb, w1_b, w2_b, ln2_g_s, ln2_b_s, l)
    return xf.reshape(batch, seq, d)
```

```python
import functools
import math

import jax
import jax.numpy as jnp
from jax import lax
from jax.experimental import pallas as pl
from jax.experimental.pallas import tpu as pltpu

D_MODEL = 1024
DEPTH = 4
ROPE_THETA = 500000.0
MLA_HEADS = 8
MLA_NOPE = 64
MLA_ROPE = 32
MLA_V = 64
MLA_Q_LORA = 384
MLA_KV_LORA = 256
DIFF_HEADS = 8
DIFF_D = 64
DIFF_ROT = DIFF_D // 4
MEM_HEADS = 4
MEM_HD = 128
N_BRANCH = 3
D_FF = 4 * D_MODEL
DEEPNORM_ALPHA = (2 * DEPTH) ** 0.25
LN_EPS = 1e-5
MLA_OUT = MLA_HEADS * MLA_V
DIFF_QK = DIFF_HEADS * 2 * DIFF_D
DIFF_OUT = DIFF_HEADS * 2 * DIFF_D
MEM_OUT = MEM_HEADS * MEM_HD

LANE = 128
V7X_VMEM_BYTES = 64 * 1024 * 1024

MLA_ROPE_LO = MLA_NOPE
MLA_ROPE_HALF = MLA_ROPE // 2
DIFF_ROPE_HALF = DIFF_ROT // 2

BF16 = jnp.bfloat16
F32 = jnp.float32
LOG2E = math.log2(math.e)


def _dot(a, b):
    return jnp.dot(a, b, preferred_element_type=F32)


def _dot_nt(a, b):
    return lax.dot_general(a, b, (((1,), (1,)), ((), ())), preferred_element_type=F32)


def _compiler_params(semantics, vmem_bytes):
    limit = min(int(vmem_bytes), V7X_VMEM_BYTES - 8 * 1024 * 1024)
    return pltpu.CompilerParams(dimension_semantics=semantics, vmem_limit_bytes=limit)


def _nbytes(shape, dtype):
    return math.prod(shape) * jnp.dtype(dtype).itemsize


def _vmem_estimate(blocks, temps):
    return 2 * sum(_nbytes(s, d) for s, d in blocks) + sum(_nbytes(s, d) for s, d in temps) + (4 << 20)


def _rope_rotate(y, cos, sin_signed, x1_mask, half):
    partner = jnp.where(x1_mask, pltpu.roll(y, LANE - half, 1), pltpu.roll(y, half, 1))
    return y * cos + partner * sin_signed


def _x1_mask_mla(rows):
    lane = lax.broadcasted_iota(jnp.int32, (rows, LANE), 1)
    return (lane >= MLA_ROPE_LO) & (lane < MLA_ROPE_LO + MLA_ROPE_HALF)


def _x1_mask_diff(rows):
    lane = lax.broadcasted_iota(jnp.int32, (rows, LANE), 1) % DIFF_D
    return lane < DIFF_ROPE_HALF


def _layernorm(z, g, b):
    mu = jnp.mean(z, axis=-1, keepdims=True)
    zc = z - mu
    var = jnp.mean(zc * zc, axis=-1, keepdims=True)
    return zc * lax.rsqrt(var + LN_EPS) * g + b


def _rope_table_kernel(pos_ref, inv_ref, sgn_ref, cos_ref, sin_ref):
    ang = pos_ref[...].astype(F32) * inv_ref[...]
    cos_ref[...] = jnp.cos(ang)
    sin_ref[...] = jnp.sin(ang) * sgn_ref[...]


def _rope_tables(pos, inv_lane, sgn_lane, tm=2048):
    t = pos.shape[0]
    return pl.pallas_call(
        _rope_table_kernel,
        grid=(t // tm,),
        in_specs=[
            pl.BlockSpec((tm, 1), lambda i: (i, 0)),
            pl.BlockSpec((1, LANE), lambda i: (0, 0)),
            pl.BlockSpec((1, LANE), lambda i: (0, 0)),
        ],
        out_specs=[pl.BlockSpec((tm, LANE), lambda i: (i, 0))] * 2,
        out_shape=[jax.ShapeDtypeStruct((t, LANE), F32)] * 2,
        compiler_params=_compiler_params(("parallel",), 32 << 20),
        name="rope_tables",
    )(pos, inv_lane, sgn_lane)


def _mla_proj_kernel(x_ref, wa_ref, gq_ref, gkv_ref, wuq_ref, wuqp_ref, wuk_ref, wuv_ref, cos_ref, sin_ref,
                     q_ref, k_ref, v_ref, *, scale):
    tm = x_ref.shape[0]
    h = _dot(x_ref[...], wa_ref[...])
    cq = h[:, :MLA_Q_LORA]
    ckv = h[:, MLA_Q_LORA:MLA_Q_LORA + MLA_KV_LORA]
    kpe = h[:, MLA_Q_LORA + MLA_KV_LORA:]

    def rms(v, g):
        return (v * lax.rsqrt(jnp.mean(v * v, axis=-1, keepdims=True) + 1e-6) * g).astype(BF16)

    cqn = rms(cq, gq_ref[...])
    ckvn = rms(ckv, gkv_ref[...])
    cos = cos_ref[...]
    sin = sin_ref[...]
    kpe_rot = _rope_rotate(kpe, cos, sin, _x1_mask_mla(tm), MLA_ROPE_HALF)

    q = _dot(cqn, wuq_ref[...])
    q_partner = _dot(cqn, wuqp_ref[...])
    kn = _dot(ckvn, wuk_ref[...])
    cos_q = cos * scale
    sin_q = sin * scale
    for hd in range(MLA_HEADS):
        sl = slice(hd * LANE, (hd + 1) * LANE)
        q_ref[:, sl] = (q[:, sl] * cos_q + q_partner[:, sl] * sin_q).astype(BF16)
        k_ref[:, sl] = (kn[:, sl] + kpe_rot).astype(BF16)
    v = _dot(ckvn, wuv_ref[...]).astype(BF16)
    ones = jnp.ones((tm, LANE), BF16)
    for pair in range(MLA_HEADS // 2):
        v_ref[:, 2 * pair * LANE:(2 * pair + 1) * LANE] = v[:, pair * LANE:(pair + 1) * LANE]
        v_ref[:, (2 * pair + 1) * LANE:(2 * pair + 2) * LANE] = ones


def _layer_block(stack, layer):
    return pl.BlockSpec((None,) + stack.shape[1:], lambda *_: (layer, 0, 0), pipeline_mode=pl.Buffered(1))


def _mla_proj(xb, wa, gq, gkv, wuq, wuqp, wuk, wuv, cos, sin, layer, tm=1024):
    t = xb.shape[0]
    na = wa.shape[2]
    nq = MLA_HEADS * LANE
    nv = MLA_HEADS * LANE
    rows = lambda n: pl.BlockSpec((tm, n), lambda i: (i, 0))
    stacks = (wa, gq, gkv, wuq, wuqp, wuk, wuv)
    vmem = _vmem_estimate(
        [((tm, D_MODEL), BF16)] + [(a.shape[1:], BF16) for a in (wa, wuq, wuqp, wuk, wuv)]
        + [((tm, LANE), F32), ((tm, LANE), F32), ((tm, nq), BF16), ((tm, nq), BF16), ((tm, nv), BF16)],
        [((tm, na), F32)] + [((tm, nq), F32)] * 4)
    return pl.pallas_call(
        functools.partial(_mla_proj_kernel, scale=LOG2E * (MLA_NOPE + MLA_ROPE) ** -0.5),
        grid=(t // tm,),
        in_specs=[rows(D_MODEL)] + [_layer_block(a, layer) for a in stacks] + [rows(LANE), rows(LANE)],
        out_specs=[rows(nq), rows(nq), rows(nv)],
        out_shape=[jax.ShapeDtypeStruct((t, nq), BF16), jax.ShapeDtypeStruct((t, nq), BF16),
                   jax.ShapeDtypeStruct((t, nv), BF16)],
        compiler_params=_compiler_params(("parallel",), vmem),
        name="mla_proj",
    )(xb, *stacks, cos, sin)


def _branch_proj_kernel(x_ref, w_ref, cos_ref, sin_ref, qd_ref, kd_ref, vd_ref, qe_ref, *, q_scale, mem_scale):
    tm = x_ref.shape[0]
    x = x_ref[...]
    cos = cos_ref[...]
    sin = sin_ref[...]
    x1 = _x1_mask_diff(tm)

    def rotated(y, scale, o_ref):
        for hd in range(DIFF_HEADS):
            sl = slice(hd * LANE, (hd + 1) * LANE)
            yh = _rope_rotate(y[:, sl], cos, sin, x1, DIFF_ROPE_HALF)
            o_ref[:, sl] = (yh if scale == 1.0 else yh * scale).astype(o_ref.dtype)

    rotated(_dot(x, w_ref[:, :DIFF_QK]), q_scale, qd_ref)
    rotated(_dot(x, w_ref[:, DIFF_QK:2 * DIFF_QK]), 1.0, kd_ref)
    v = _dot(x, w_ref[:, 2 * DIFF_QK:2 * DIFF_QK + DIFF_OUT]).astype(vd_ref.dtype)
    ones = jnp.ones((tm, LANE), vd_ref.dtype)
    for hd in range(DIFF_HEADS):
        vd_ref[:, 2 * hd * LANE:(2 * hd + 1) * LANE] = v[:, hd * LANE:(hd + 1) * LANE]
        vd_ref[:, (2 * hd + 1) * LANE:(2 * hd + 2) * LANE] = ones
    qe_ref[...] = (_dot(x, w_ref[:, 2 * DIFF_QK + DIFF_OUT:]) * mem_scale).astype(qe_ref.dtype)


def _branch_proj(xb, w_stack, cos, sin, layer, tm=1024):
    t, kdim = xb.shape
    n = w_stack.shape[2]
    rows = lambda width: pl.BlockSpec((tm, width), lambda i: (i, 0))
    widths = (DIFF_QK, DIFF_QK, 2 * DIFF_OUT, MEM_OUT)
    vmem = _vmem_estimate(
        [((tm, kdim), BF16), ((kdim, n), BF16), ((tm, LANE), F32), ((tm, LANE), F32)]
        + [((tm, width), BF16) for width in widths],
        [((tm, DIFF_QK), F32)] * 6)
    return pl.pallas_call(
        functools.partial(_branch_proj_kernel, q_scale=LOG2E * DIFF_D ** -0.5, mem_scale=MEM_HD ** -0.5),
        grid=(t // tm,),
        in_specs=[rows(kdim), _layer_block(w_stack, layer), rows(LANE), rows(LANE)],
        out_specs=[rows(width) for width in widths],
        out_shape=[jax.ShapeDtypeStruct((t, width), BF16) for width in widths],
        compiler_params=_compiler_params(("parallel",), vmem),
        name="branch_proj",
    )(xb, w_stack, cos, sin)


def _mem_kv_kernel(x_ref, w_ref, o_ref):
    o_ref[...] = _dot(x_ref[...], w_ref[...]).astype(o_ref.dtype)


def _mem_kv_proj(memb, w_stack, tm=1024):
    t, kdim = memb.shape
    depth, _, n = w_stack.shape
    vmem = _vmem_estimate([((tm, kdim), BF16), ((kdim, n), BF16), ((tm, n), BF16)], [((tm, n), F32)] * 2)
    return pl.pallas_call(
        _mem_kv_kernel,
        grid=(depth, t // tm),
        in_specs=[pl.BlockSpec((tm, kdim), lambda l, i: (i, 0)),
                  pl.BlockSpec((None, kdim, n), lambda l, i: (l, 0, 0))],
        out_specs=pl.BlockSpec((None, tm, n), lambda l, i: (l, i, 0)),
        out_shape=jax.ShapeDtypeStruct((depth, t, n), BF16),
        compiler_params=_compiler_params(("parallel", "parallel"), vmem),
        name="mem_kv_proj",
    )(memb, w_stack)


def _softmax_pv(q, k, v_ext):
    return _normalized_pv(_softmax_numerators(q, k), v_ext)


def _softmax_numerators(q, k):
    return _numerators_from_scores(_dot_nt(q, k))


def _numerators_from_scores(s):
    m = jnp.max(s, axis=-1, keepdims=True)
    return jnp.exp2((s - m).astype(BF16))


def _normalized_pv(p, v_ext):
    pv = _dot(p, v_ext)
    return pv[:, :LANE] / pv[:, LANE:]


ATTN_ROWS = 256


def _mla_attn_kernel(q_ref, k_ref, v_ref, o_ref, *, pairs):
    first = lax.broadcasted_iota(jnp.int32, (ATTN_ROWS, LANE), 1) < MLA_V
    for sub in range(q_ref.shape[0] // ATTN_ROWS):
        rows = slice(sub * ATTN_ROWS, (sub + 1) * ATTN_ROWS)
        for pr in range(pairs):
            v_ext = v_ref[:, 2 * pr * LANE:(2 * pr + 2) * LANE]
            outs = []
            for j in range(2):
                sl = slice((2 * pr + j) * LANE, (2 * pr + j + 1) * LANE)
                outs.append(_softmax_pv(q_ref[rows, sl], k_ref[:, sl], v_ext))
            o_ref[rows, pr * LANE:(pr + 1) * LANE] = jnp.where(first, outs[0], outs[1]).astype(o_ref.dtype)


def _mla_attn(q, k, v, batch, seq, tq=1024, pairs=4):
    t = q.shape[0]
    nq = seq // tq
    wide = 2 * pairs * LANE
    grid = (batch, MLA_HEADS // (2 * pairs), nq)
    vmem = _vmem_estimate(
        [((tq, wide), BF16), ((seq, wide), BF16), ((seq, wide), BF16), ((tq, pairs * LANE), BF16)],
        ([((tq, seq), F32)] * 2 + [((tq, seq), BF16)] * 2) * 2 * pairs)
    return pl.pallas_call(
        functools.partial(_mla_attn_kernel, pairs=pairs),
        grid=grid,
        in_specs=[
            pl.BlockSpec((tq, wide), lambda b, g, i: (b * nq + i, g)),
            pl.BlockSpec((seq, wide), lambda b, g, i: (b, g)),
            pl.BlockSpec((seq, wide), lambda b, g, i: (b, g)),
        ],
        out_specs=pl.BlockSpec((tq, pairs * LANE), lambda b, g, i: (b * nq + i, g)),
        out_shape=jax.ShapeDtypeStruct((t, MLA_OUT), BF16),
        compiler_params=_compiler_params(("parallel", "parallel", "arbitrary"), vmem),
        name="mla_attn",
    )(q, k, v)


def _diff_attn_kernel(q_ref, k_ref, v_ref, lam_ref, g_ref, o_ref, *, lambda_init, heads):
    tq = q_ref.shape[0]
    lam = lam_ref[...]
    lam_a = jnp.sum(lam[0:1] * lam[1:2], axis=-1, keepdims=True)
    lam_b = jnp.sum(lam[2:3] * lam[3:4], axis=-1, keepdims=True)
    lambda_full = jnp.exp(lam_a) - jnp.exp(lam_b) + lambda_init
    map0 = lax.broadcasted_iota(jnp.int32, (ATTN_ROWS, LANE), 1) < DIFF_D
    gain = g_ref[...] * (1.0 - lambda_init)
    for sub in range(tq // ATTN_ROWS):
        rows = slice(sub * ATTN_ROWS, (sub + 1) * ATTN_ROWS)
        for hd in range(heads):
            sl = slice(hd * LANE, (hd + 1) * LANE)
            q = q_ref[rows, sl]
            k = k_ref[:, sl]
            v_ext = v_ref[:, 2 * hd * LANE:(2 * hd + 2) * LANE]
            zero = jnp.zeros_like(q)
            o = _softmax_pv(jnp.where(map0, q, zero), k, v_ext) - lambda_full * _softmax_pv(
                jnp.where(map0, zero, q), k, v_ext)
            o = o * lax.rsqrt(jnp.mean(o * o, axis=-1, keepdims=True) + 1e-5) * gain
            o_ref[rows, sl] = o.astype(o_ref.dtype)


def _diff_attn(q, k, v, lam_stack, subln_stack, layer, lambda_init, batch, seq, tq=512, heads=8):
    t = q.shape[0]
    nq = seq // tq
    wide = heads * LANE
    vmem = _vmem_estimate(
        [((tq, wide), BF16), ((seq, wide), BF16), ((seq, 2 * wide), BF16), ((tq, wide), BF16)],
        ([((tq, seq), F32)] * 2 + [((tq, seq), BF16)] * 2) * 2 * heads)
    return pl.pallas_call(
        functools.partial(_diff_attn_kernel, lambda_init=lambda_init, heads=heads),
        grid=(batch, DIFF_HEADS // heads, nq),
        in_specs=[
            pl.BlockSpec((tq, wide), lambda b, h, i: (b * nq + i, h)),
            pl.BlockSpec((seq, wide), lambda b, h, i: (b, h)),
            pl.BlockSpec((seq, 2 * wide), lambda b, h, i: (b, h)),
            _layer_block(lam_stack, layer),
            _layer_block(subln_stack, layer),
        ],
        out_specs=pl.BlockSpec((tq, wide), lambda b, h, i: (b * nq + i, h)),
        out_shape=jax.ShapeDtypeStruct((t, DIFF_OUT), BF16),
        compiler_params=_compiler_params(("parallel", "parallel", "arbitrary"), vmem),
        name="diff_attn",
    )(q, k, v, lam_stack, subln_stack)


def _mem_attn_kernel(q_ref, kv_ref, o_ref):
    for hd in range(MEM_HEADS):
        sl = slice(hd * MEM_HD, (hd + 1) * MEM_HD)
        vsl = slice(MEM_OUT + hd * MEM_HD, MEM_OUT + (hd + 1) * MEM_HD)
        s = _dot_nt(q_ref[:, sl], kv_ref[:, sl])
        p = jnp.exp(s - jnp.max(s, axis=-1, keepdims=True))
        inv_l = 1.0 / jnp.sum(p, axis=-1, keepdims=True)
        o_ref[:, sl] = (_dot(p.astype(BF16), kv_ref[:, vsl]) * inv_l).astype(o_ref.dtype)


def _mem_attn(q, kv_stack, layer, batch, seq, mem_len, tq=2048):
    t = q.shape[0]
    nq = seq // tq
    vmem = _vmem_estimate(
        [((tq, MEM_OUT), BF16), ((mem_len, 2 * MEM_OUT), BF16), ((tq, MEM_OUT), BF16)],
        [((tq, mem_len), F32)] * 3 + [((tq, MEM_OUT), F32)])
    return pl.pallas_call(
        _mem_attn_kernel,
        grid=(batch, nq),
        in_specs=[
            pl.BlockSpec((tq, MEM_OUT), lambda b, i: (b * nq + i, 0)),
            pl.BlockSpec((None, mem_len, 2 * MEM_OUT), lambda b, i: (layer, b, 0)),
        ],
        out_specs=pl.BlockSpec((tq, MEM_OUT), lambda b, i: (b * nq + i, 0)),
        out_shape=jax.ShapeDtypeStruct((t, MEM_OUT), BF16),
        compiler_params=_compiler_params(("parallel", "arbitrary"), vmem),
        name="mem_attn",
    )(q, kv_stack)


MERGE_ROW_CHUNK = 256


def _merge_kernel(*refs, pieces):
    n_in = sum(pieces)
    x_ref, xb_ref = refs[:2]
    branch_refs = refs[2:2 + n_in]
    wg_ref, bg_ref, wb_ref, wo_ref, g_ref, b_ref, y_ref, yb_ref = refs[2 + n_in:]
    chunk = MERGE_ROW_CHUNK
    for c in range(x_ref.shape[0] // chunk):
        rows = slice(c * chunk, (c + 1) * chunk)
        xb = xb_ref[rows, :]
        merged = None
        row = 0
        piece = 0
        for i, count in enumerate(pieces):
            sl = slice(i * D_MODEL, (i + 1) * D_MODEL)
            gate = jax.nn.sigmoid(_dot(xb, wg_ref[:, sl]) + bg_ref[i:i + 1, :])
            proj = None
            for o_ref in branch_refs[piece:piece + count]:
                width = o_ref.shape[1]
                part = _dot(o_ref[rows, :], wb_ref[row:row + width, :])
                proj = part if proj is None else proj + part
                row += width
            piece += count
            merged = gate * proj if merged is None else merged + gate * proj
        z = DEEPNORM_ALPHA * x_ref[rows, :] + _dot(merged.astype(BF16), wo_ref[...])
        y = _layernorm(z, g_ref[...], b_ref[...])
        y_ref[rows, :] = y
        yb_ref[rows, :] = y.astype(BF16)


def _merge(x, xb, branches, wg_stack, bg_stack, wb_stack, wo_stack, g_stack, b_stack, layer, tm=1024):
    t = x.shape[0]
    rows = lambda n: pl.BlockSpec((tm, n), lambda i: (i, 0))
    flat = [a for group in branches for a in group]
    vmem = _vmem_estimate(
        [((tm, D_MODEL), F32), ((tm, D_MODEL), BF16)] + [((tm, a.shape[1]), BF16) for a in flat]
        + [(wg_stack.shape[1:], BF16), (wb_stack.shape[1:], BF16), (wo_stack.shape[1:], BF16),
           ((tm, D_MODEL), F32), ((tm, D_MODEL), BF16)],
        [((tm, D_MODEL), F32)] * 5)
    return pl.pallas_call(
        functools.partial(_merge_kernel, pieces=tuple(len(group) for group in branches)),
        grid=(t // tm,),
        in_specs=[rows(D_MODEL), rows(D_MODEL)] + [rows(a.shape[1]) for a in flat]
        + [_layer_block(wg_stack, layer), _layer_block(bg_stack, layer),
           _layer_block(wb_stack, layer), _layer_block(wo_stack, layer),
           _layer_block(g_stack, layer), _layer_block(b_stack, layer)],
        out_specs=[rows(D_MODEL), rows(D_MODEL)],
        out_shape=[jax.ShapeDtypeStruct((t, D_MODEL), F32), jax.ShapeDtypeStruct((t, D_MODEL), BF16)],
        compiler_params=_compiler_params(("parallel",), vmem),
        name="merge",
    )(x, xb, *flat, wg_stack, bg_stack, wb_stack, wo_stack, g_stack, b_stack)


MLP_ROW_CHUNK = 256
MLP_HIDDEN_CHUNK = 1024


def _mlp_kernel(x_ref, xb_ref, w1_ref, w2_ref, g_ref, b_ref, y_ref, yb_ref):
    for c in range(x_ref.shape[0] // MLP_ROW_CHUNK):
        rows = slice(c * MLP_ROW_CHUNK, (c + 1) * MLP_ROW_CHUNK)
        xb = xb_ref[rows, :]
        acc = None
        for f in range(D_FF // MLP_HIDDEN_CHUNK):
            cols = slice(f * MLP_HIDDEN_CHUNK, (f + 1) * MLP_HIDDEN_CHUNK)
            h = jnp.maximum(_dot(xb, w1_ref[:, cols]), 0.0)
            part = _dot((h * h).astype(BF16), w2_ref[cols, :])
            acc = part if acc is None else acc + part
        y = _layernorm(DEEPNORM_ALPHA * x_ref[rows, :] + acc, g_ref[...], b_ref[...])
        y_ref[rows, :] = y
        yb_ref[rows, :] = y.astype(BF16)


def _mlp(x, xb, w1_stack, w2_stack, g_stack, b_stack, layer, tm=1024):
    t = x.shape[0]
    resident = lambda stack: _layer_block(stack, layer)
    vmem = (_vmem_estimate(
        [((tm, D_MODEL), F32), ((tm, D_MODEL), BF16), ((tm, D_MODEL), F32), ((tm, D_MODEL), BF16)],
        [((MLP_ROW_CHUNK, MLP_HIDDEN_CHUNK), F32)] * 4 + [((MLP_ROW_CHUNK, D_MODEL), F32)] * 4)
        + _nbytes(w1_stack.shape[1:], BF16) + _nbytes(w2_stack.shape[1:], BF16))
    return pl.pallas_call(
        _mlp_kernel,
        grid=(t // tm,),
        in_specs=[
            pl.BlockSpec((tm, D_MODEL), lambda i: (i, 0)),
            pl.BlockSpec((tm, D_MODEL), lambda i: (i, 0)),
            resident(w1_stack),
            resident(w2_stack),
            _layer_block(g_stack, layer),
            _layer_block(b_stack, layer),
        ],
        out_specs=[pl.BlockSpec((tm, D_MODEL), lambda i: (i, 0))] * 2,
        out_shape=[jax.ShapeDtypeStruct((t, D_MODEL), F32), jax.ShapeDtypeStruct((t, D_MODEL), BF16)],
        compiler_params=_compiler_params(("parallel",), vmem),
        name="mlp",
    )(x, xb, w1_stack, w2_stack, g_stack, b_stack)


def _lane_patterns():
    lane = jnp.arange(LANE)
    inv_m = ROPE_THETA ** (-jnp.arange(0, MLA_ROPE, 2, dtype=F32) / MLA_ROPE)
    rel = lane - MLA_ROPE_LO
    in_rope = (rel >= 0) & (rel < MLA_ROPE)
    inv_mla = jnp.where(in_rope, inv_m[jnp.clip(rel, 0, MLA_ROPE - 1) % MLA_ROPE_HALF], 0.0)
    sgn_mla = jnp.where(in_rope, jnp.where(rel < MLA_ROPE_HALF, -1.0, 1.0), 0.0)
    inv_d = ROPE_THETA ** (-jnp.arange(0, DIFF_ROT, 2, dtype=F32) / DIFF_ROT)
    rel = lane % DIFF_D
    in_rope = rel < DIFF_ROT
    inv_diff = jnp.where(in_rope, inv_d[rel % DIFF_ROPE_HALF], 0.0)
    sgn_diff = jnp.where(in_rope, jnp.where(rel < DIFF_ROPE_HALF, -1.0, 1.0), 0.0)
    row = lambda a: a.astype(F32).reshape(1, LANE)
    return row(inv_mla), row(sgn_mla), row(inv_diff), row(sgn_diff)


def _weight_stacks(w_in, mla_w_uq, mla_w_ukv):
    depth, d, _ = w_in.shape
    lat = MLA_Q_LORA + MLA_KV_LORA
    branch_lo = lat + MLA_ROPE
    gate_lo = branch_lo + 2 * DIFF_QK + DIFF_OUT + MEM_OUT
    w_a = jnp.concatenate(
        [w_in[:, :, :lat], jnp.zeros((depth, d, MLA_ROPE_LO), F32), w_in[:, :, lat:branch_lo],
         jnp.zeros((depth, d, LANE - MLA_ROPE_LO - MLA_ROPE), F32)], axis=2).astype(BF16)
    w_branch_in = w_in[:, :, branch_lo:gate_lo].astype(BF16)
    w_gate = w_in[:, :, gate_lo:].astype(BF16)

    qd = MLA_NOPE + MLA_ROPE
    w_uq = jnp.pad(mla_w_uq.reshape(depth, MLA_Q_LORA, MLA_HEADS, qd), ((0, 0), (0, 0), (0, 0), (0, LANE - qd)))
    lane = jnp.arange(LANE)
    rel = lane - MLA_ROPE_LO
    in_rope = (rel >= 0) & (rel < MLA_ROPE)
    partner = jnp.where(rel < MLA_ROPE_HALF, lane + MLA_ROPE_HALF, lane - MLA_ROPE_HALF)
    w_uq_partner = jnp.where(in_rope, jnp.take(w_uq, jnp.clip(partner, 0, LANE - 1), axis=3), 0.0)
    flat = lambda a: a.reshape(depth, a.shape[1], -1).astype(BF16)
    ukv = mla_w_ukv.reshape(depth, MLA_KV_LORA, MLA_HEADS, MLA_NOPE + MLA_V)
    w_uk = jnp.pad(ukv[..., :MLA_NOPE], ((0, 0), (0, 0), (0, 0), (0, LANE - MLA_NOPE)))
    return w_a, w_branch_in, w_gate, flat(w_uq), flat(w_uq_partner), flat(w_uk), flat(ukv[..., MLA_NOPE:])


def kernel(x, mem, positions, w_in, b_gate, mla_q_norm, mla_kv_norm, mla_w_uq, mla_w_ukv, diff_lambda, diff_subln,
           mem_w_kv, w_branch, w_out, ln1_g, ln1_b, mlp_w1, mlp_w2, ln2_g, ln2_b):
    batch, seq, d = x.shape
    mem_len = mem.shape[1]
    t = batch * seq
    depth = w_in.shape[0]

    inv_mla, sgn_mla, inv_diff, sgn_diff = _lane_patterns()
    pos = positions.reshape(t, 1)
    cos_m, sin_m = _rope_tables(pos, inv_mla, sgn_mla)
    cos_d, sin_d = _rope_tables(pos, inv_diff, sgn_diff)

    xf = x.reshape(t, d)
    xb = xf.astype(BF16)
    memb = mem.reshape(batch * mem_len, d).astype(BF16)

    w_a, w_branch_in, w_gate, w_uq, w_uq_partner, w_uk, w_uv = _weight_stacks(w_in, mla_w_uq, mla_w_ukv)
    mem_w_kv_b, w_branch_b, w_out_b, w1_b, w2_b = (
        a.astype(BF16) for a in (mem_w_kv, w_branch, w_out, mlp_w1, mlp_w2))
    stack_rows = lambda a: a.reshape(depth, 1, -1)
    q_norm_s, kv_norm_s, subln_s, ln1_g_s, ln1_b_s, ln2_g_s, ln2_b_s = map(
        stack_rows, (mla_q_norm, mla_kv_norm, diff_subln, ln1_g, ln1_b, ln2_g, ln2_b))

    kv_mem = _mem_kv_proj(memb, mem_w_kv_b)

    for l in range(depth):
        lambda_init = 0.8 - 0.6 * math.exp(-0.3 * l)
        q_m, k_m, v_m = _mla_proj(xb, w_a, q_norm_s, kv_norm_s, w_uq, w_uq_partner, w_uk, w_uv, cos_m, sin_m, l)
        o_mla = _mla_attn(q_m, k_m, v_m, batch, seq)

        q_d, k_d, v_d, q_e = _branch_proj(xb, w_branch_in, cos_d, sin_d, l)
        o_diff = _diff_attn(q_d, k_d, v_d, diff_lambda, subln_s, l, lambda_init, batch, seq)

        o_mem = _mem_attn(q_e, kv_mem, l, batch, seq, mem_len)

        xf, xb = _merge(xf, xb, ((o_mla,), (o_diff,), (o_mem,)), w_gate, b_gate, w_branch_b, w_out_b,
                        ln1_g_s, ln1_b_s, l)
        xf, xb = _mlp(xf, xb, w1_b, w2_b, ln2_g_s, ln2_b_s, l)
    return xf.reshape(batch, seq, d)
```
